```python
import jax, jax.numpy as jnp
from jax import lax
import numpy as np

D_MODEL = 1024
BATCH = 32
SEQ = 2048
DEPTH = 1

HG_HEADS = 4
HG_DIM = 128
HG_WIDTH = HG_HEADS * HG_DIM
HG_CHUNK = 64
MB_HEADS = 8
MB_HEAD_DIM = 64
MB_WIDTH = MB_HEADS * MB_HEAD_DIM
MB_BLOCK = 256
MB_TOPK = 3
MB_QCHUNK = 4
ROPE_THETA = 10000.0
D_FF = 2816
CONV_WIDTH = 3
NORM_EPS = 1e-6
IN_WIDTHS = (HG_WIDTH, HG_WIDTH, HG_WIDTH, HG_WIDTH, MB_WIDTH, MB_WIDTH, MB_WIDTH, D_MODEL, D_MODEL)
D_IN = sum(IN_WIDTHS)
IN_SPLITS = tuple(int(s) for s in np.cumsum(IN_WIDTHS)[:-1])

kernel_name = "hybrid_hgrn2_moba_convffn_block"


def rms_norm(x, g):
    xf = x.astype(jnp.float32)
    y = xf * lax.rsqrt(jnp.mean(xf * xf, axis=-1, keepdims=True) + NORM_EPS)
    return (y * g.astype(jnp.float32)).astype(x.dtype)


def rope(x, pos):
    d = x.shape[-1]
    half = d // 2
    inv = 1.0 / (ROPE_THETA ** (jnp.arange(half, dtype=jnp.float32) * 2.0 / d))
    ang = pos.astype(jnp.float32)[:, None] * inv[None, :]
    cos = jnp.cos(ang)[:, None, :]
    sin = jnp.sin(ang)[:, None, :]
    xf = x.astype(jnp.float32)
    x1, x2 = xf[..., :half], xf[..., half:]
    return jnp.concatenate([x1 * cos - x2 * sin, x2 * cos + x1 * sin], axis=-1).astype(x.dtype)


def _to_chunks(t):
    b, s, h, e = t.shape
    return t.reshape(b, s // HG_CHUNK, HG_CHUNK, h, e).transpose(1, 0, 3, 2, 4)


def hgrn2_chunkwise(q, k, v, logf):
    b, s, h, dk = q.shape
    dv = v.shape[-1]
    causal = jnp.tril(jnp.ones((HG_CHUNK, HG_CHUNK), dtype=bool))

    def step(state, inp):
        qc, kc, vc, lf = inp
        cum = jnp.cumsum(lf, axis=2)
        o_inter = jnp.einsum('bhck,bhkv->bhcv', qc * jnp.exp(cum), state)
        diff = cum[:, :, :, None, :] - cum[:, :, None, :, :]
        decay = jnp.exp(jnp.where(causal[None, None, :, :, None], diff, -jnp.inf))
        attn = jnp.einsum('bhtk,bhtsk,bhsk->bhts', qc, decay, kc)
        o = o_inter + jnp.einsum('bhts,bhsv->bhtv', attn, vc)
        last = cum[:, :, -1:, :]
        new_state = jnp.exp(last[:, :, 0, :])[..., None] * state + jnp.einsum(
            'bhsk,bhsv->bhkv', kc * jnp.exp(last - cum), vc)
        return new_state, o

    state0 = jnp.zeros((b, h, dk, dv), jnp.float32)
    _, o = lax.scan(step, state0, (_to_chunks(q), _to_chunks(k), _to_chunks(v), _to_chunks(logf)))
    return o.transpose(1, 0, 3, 2, 4).reshape(b, s, h, dv)


def moba_attention(q, k, v):
    b, s, h, d = q.shape
    nb = -(-s // MB_BLOCK)
    pad = nb * MB_BLOCK - s
    scale = 1.0 / float(np.sqrt(d))
    qt = q.transpose(0, 2, 1, 3)
    kb = jnp.pad(k.transpose(0, 2, 1, 3), ((0, 0), (0, 0), (0, pad), (0, 0))).reshape(b, h, nb, MB_BLOCK, d)
    vb = jnp.pad(v.transpose(0, 2, 1, 3), ((0, 0), (0, 0), (0, pad), (0, 0))).reshape(b, h, nb, MB_BLOCK, d)
    kmean = jnp.mean(kb.astype(jnp.float32), axis=3)

    pos = jnp.arange(s)
    qblk = pos // MB_BLOCK
    gate = jnp.einsum('bhsd,bhnd->bhsn', qt.astype(jnp.float32), kmean)
    fully_past = jnp.arange(nb)[None, :] < qblk[:, None]
    gate = jnp.where(fully_past[None, None], gate, -jnp.inf)
    n_sel = max(1, min(MB_TOPK, nb - 1))
    _, idx = lax.top_k(gate, n_sel)
    valid = jnp.arange(n_sel)[None, :] < qblk[:, None]

    nq = s // MB_QCHUNK
    q_ch = qt.reshape(b, h, nq, MB_QCHUNK, d).transpose(2, 0, 1, 3, 4)
    i_ch = idx.reshape(b, h, nq, MB_QCHUNK, n_sel).transpose(2, 0, 1, 3, 4)
    v_ch = valid.reshape(nq, MB_QCHUNK, n_sel)
    p_ch = pos.reshape(nq, MB_QCHUNK)
    bi = jnp.arange(b)[:, None, None, None]
    hi = jnp.arange(h)[None, :, None, None]

    def attend(inp):
        qc, ic, vc, pc = inp
        kg = kb[bi, hi, ic]
        vg = vb[bi, hi, ic]
        own = pc[0] // MB_BLOCK
        k_own = lax.dynamic_index_in_dim(kb, own, axis=2, keepdims=False)
        v_own = lax.dynamic_index_in_dim(vb, own, axis=2, keepdims=False)
        s_sel = jnp.einsum('bhqd,bhqrkd->bhqrk', qc, kg).astype(jnp.float32) * scale
        s_sel = jnp.where(vc[None, None, :, :, None], s_sel, -jnp.inf)
        s_own = jnp.einsum('bhqd,bhkd->bhqk', qc, k_own).astype(jnp.float32) * scale
        key_pos = own * MB_BLOCK + jnp.arange(MB_BLOCK)
        s_own = jnp.where((key_pos[None, :] <= pc[:, None])[None, None], s_own, -jnp.inf)
        logits = jnp.concatenate([s_sel.reshape(b, h, MB_QCHUNK, n_sel * MB_BLOCK), s_own], axis=-1)
        p = jax.nn.softmax(logits, axis=-1)
        p_sel = p[..., :n_sel * MB_BLOCK].reshape(b, h, MB_QCHUNK, n_sel, MB_BLOCK).astype(vg.dtype)
        p_own = p[..., n_sel * MB_BLOCK:].astype(v_own.dtype)
        return (jnp.einsum('bhqrk,bhqrkd->bhqd', p_sel, vg)
                + jnp.einsum('bhqk,bhkd->bhqd', p_own, v_own))

    o = lax.map(attend, (q_ch, i_ch, v_ch, p_ch))
    return o.transpose(1, 0, 3, 2, 4).reshape(b, s, h, d)


def causal_dwconv(u, w, bias):
    s = u.shape[1]
    up = jnp.pad(u, ((0, 0), (CONV_WIDTH - 1, 0), (0, 0)))
    out = bias
    for j in range(CONV_WIDTH):
        out = out + up[:, j:j + s] * w[j]
    return out


def setup_inputs(seed: int = 0) -> dict:
    key = jax.random.key(seed)
    ks = jax.random.split(key, 16)
    f32 = jnp.float32

    def nrm(k, shape, scale):
        return jax.random.normal(k, shape, f32) * scale

    return {
        "x": nrm(ks[0], (BATCH, SEQ, D_MODEL), 1.0),
        "norm1_g": 1.0 + nrm(ks[1], (DEPTH, D_MODEL), 0.02),
        "w_in": nrm(ks[2], (DEPTH, D_MODEL, D_IN), D_MODEL ** -0.5),
        "hg_lb_logits": nrm(ks[3], (DEPTH + 1, HG_WIDTH), 1.0),
        "hg_onorm_g": 1.0 + nrm(ks[4], (DEPTH, HG_HEADS, HG_DIM), 0.02),
        "q_norm_g": 1.0 + nrm(ks[5], (DEPTH, MB_HEAD_DIM), 0.02),
        "k_norm_g": 1.0 + nrm(ks[6], (DEPTH, MB_HEAD_DIM), 0.02),
        "w_a": nrm(ks[7], (DEPTH, HG_WIDTH, D_MODEL), HG_WIDTH ** -0.5),
        "w_b": nrm(ks[8], (DEPTH, MB_WIDTH, D_MODEL), MB_WIDTH ** -0.5),
        "w_out": nrm(ks[9], (DEPTH, D_MODEL, D_MODEL), D_MODEL ** -0.5),
        "norm2_g": 1.0 + nrm(ks[10], (DEPTH, D_MODEL), 0.02),
        "w_up": nrm(ks[11], (DEPTH, D_MODEL, 2 * D_FF), D_MODEL ** -0.5),
        "conv_w": nrm(ks[12], (DEPTH, CONV_WIDTH, D_FF), CONV_WIDTH ** -0.5),
        "conv_b": nrm(ks[13], (DEPTH, D_FF), 0.01),
        "w_down": nrm(ks[14], (DEPTH, D_FF, D_MODEL), D_FF ** -0.5),
    }


def reference(x, norm1_g, w_in, hg_lb_logits, hg_onorm_g, q_norm_g, k_norm_g, w_a, w_b, w_out,
              norm2_g, w_up, conv_w, conv_b, w_down):
    b, s, _ = x.shape
    pos = jnp.arange(s)
    lower_bounds = jnp.cumsum(jax.nn.softmax(hg_lb_logits.astype(jnp.float32), axis=0), axis=0)
    for l in range(DEPTH):
        h = rms_norm(x, norm1_g[l])
        proj = h @ w_in[l]
        hq, hf, hi, hg, mq, mk, mv, ga, gb = jnp.split(proj, IN_SPLITS, axis=-1)

        lb = lower_bounds[l]
        f = lb + (1.0 - lb) * jax.nn.sigmoid(hf.astype(jnp.float32))
        logf = jnp.log(f)
        k_in = 1.0 - f
        q_a = jax.nn.silu(hq.astype(jnp.float32))
        hd = (b, s, HG_HEADS, HG_DIM)
        o_a = hgrn2_chunkwise(q_a.reshape(hd), k_in.reshape(hd),
                              hi.astype(jnp.float32).reshape(hd), logf.reshape(hd))
        o_a = rms_norm(o_a, hg_onorm_g[l]).reshape(b, s, HG_WIDTH).astype(x.dtype) * jax.nn.silu(hg)

        md = (b, s, MB_HEADS, MB_HEAD_DIM)
        q_b = rope(rms_norm(mq.reshape(md), q_norm_g[l]), pos)
        k_b = rope(rms_norm(mk.reshape(md), k_norm_g[l]), pos)
        o_b = moba_attention(q_b, k_b, mv.reshape(md)).reshape(b, s, MB_WIDTH)

        mix = jax.nn.sigmoid(ga) * (o_a @ w_a[l]) + jax.nn.sigmoid(gb) * (o_b @ w_b[l])
        x = x + mix @ w_out[l]

        h2 = rms_norm(x, norm2_g[l])
        u, v = jnp.split(h2 @ w_up[l], 2, axis=-1)
        u = causal_dwconv(u, conv_w[l], conv_b[l])
        x = x + (jax.nn.gelu(u, approximate=False) * v) @ w_down[l]
    return x
```

```python
import functools

import numpy as np
import jax
import jax.numpy as jnp
from jax import lax
from jax.experimental import pallas as pl
from jax.experimental.pallas import tpu as pltpu

F32 = jnp.float32
BF16 = jnp.bfloat16

HG_HEADS = 4
HG_DIM = 128
HG_WIDTH = HG_HEADS * HG_DIM
HG_CHUNK = 64
MB_HEADS = 8
MB_HEAD_DIM = 64
MB_WIDTH = MB_HEADS * MB_HEAD_DIM
MB_BLOCK = 256
MB_TOPK = 3
ROPE_THETA = 10000.0
D_FF = 2816
CONV_WIDTH = 3
NORM_EPS = 1e-6

V7X_LANES = 128
V7X_SUBLANES = 8
V7X_VMEM_BYTES = 64 * 1024 * 1024

HG_SUB = 16
ROW_TILE = 512
FF_TILE = 256
NEG_INF = float("-inf")

_NT = (((1,), (1,)), ((), ()))
_TN = (((0,), (0,)), ((), ()))


def _vmem_limit(nbytes):
    return int(min(nbytes * 3 // 2 + (4 << 20), V7X_VMEM_BYTES - (4 << 20)))


def _sigmoid(x):
    return 1.0 / (1.0 + jnp.exp(-x))


def _inproj_kernel(x_ref, g1_ref, w_ref, lbl_ref, qg_ref, kg_ref, rc_ref, rs_ref, ones_ref,
                   qa_ref, logf_ref, kk_ref, vh_ref, gh_ref, qb_ref, kb_ref, vb_ref, sa_ref, sb_ref,
                   *, layer):
    x = x_ref[...]
    ms = jnp.mean(x * x, axis=-1, keepdims=True)
    h = (x * lax.rsqrt(ms + NORM_EPS) * g1_ref[...]).astype(BF16)

    def proj(c0, width):
        return jnp.dot(h, w_ref[:, c0:c0 + width], preferred_element_type=F32)

    w = HG_WIDTH
    p = proj(0, w)
    qa_ref[...] = (p * _sigmoid(p)).astype(BF16)
    a = lbl_ref[...]
    amax = jnp.max(a, axis=0, keepdims=True)
    e = jnp.exp(a - amax)
    lb = jnp.sum(e[0:layer + 1, :], axis=0, keepdims=True) / jnp.sum(e, axis=0, keepdims=True)
    p = proj(w, w)
    f = lb + (1.0 - lb) * _sigmoid(p)
    logf_ref[...] = jnp.log(f)
    kk_ref[...] = (1.0 - f).astype(BF16)
    vh_ref[...] = proj(2 * w, w).astype(BF16)
    p = proj(3 * w, w)
    gh_ref[...] = (p * _sigmoid(p)).astype(BF16)

    lane = lax.broadcasted_iota(jnp.int32, (1, V7X_LANES), 1)
    first_half = (lane % MB_HEAD_DIM) < (MB_HEAD_DIM // 2)
    rc = rc_ref[...]
    rs = rs_ref[...]
    ones_bd = ones_ref[...]

    def norm_rope(p, g_ref, out_ref, scale):
        for c in range(MB_WIDTH // V7X_LANES):
            sl = slice(c * V7X_LANES, (c + 1) * V7X_LANES)
            pc = p[:, sl]
            ss = jnp.dot((pc * pc).astype(BF16), ones_bd, preferred_element_type=F32)
            y = pc * lax.rsqrt(ss * (1.0 / MB_HEAD_DIM) + NORM_EPS) * g_ref[:, sl]
            partner = jnp.where(first_half,
                                pltpu.roll(y, V7X_LANES - MB_HEAD_DIM // 2, 1),
                                pltpu.roll(y, MB_HEAD_DIM // 2, 1))
            out_ref[:, sl] = ((y * rc + partner * rs) * scale).astype(BF16)

    base = 4 * w
    norm_rope(proj(base, MB_WIDTH), qg_ref, qb_ref, 1.0 / float(np.sqrt(MB_HEAD_DIM)))
    norm_rope(proj(base + MB_WIDTH, MB_WIDTH), kg_ref, kb_ref, 1.0)
    vb_ref[...] = proj(base + 2 * MB_WIDTH, MB_WIDTH).astype(BF16)
    base = base + 3 * MB_WIDTH
    d = sa_ref.shape[1]
    sa_ref[...] = _sigmoid(proj(base, d)).astype(BF16)
    sb_ref[...] = _sigmoid(proj(base + d, d)).astype(BF16)


def _rope_tables(seq):
    half = MB_HEAD_DIM // 2
    inv = 1.0 / (ROPE_THETA ** (jnp.arange(half, dtype=F32) * 2.0 / MB_HEAD_DIM))
    ang = jnp.arange(seq).astype(F32)[:, None] * inv[None, :]
    cos = jnp.cos(ang)
    sin = jnp.sin(ang)
    rc = jnp.concatenate([cos, cos, cos, cos], axis=-1)
    rs = jnp.concatenate([-sin, sin, -sin, sin], axis=-1)
    return rc, rs


def _inproj(x2, g1, w_in, lb_logits, qg, kg, seq, layer):
    t, d = x2.shape
    d_in = w_in.shape[1]
    tm = ROW_TILE
    rc, rs = _rope_tables(seq)
    blk = np.arange(V7X_LANES) // MB_HEAD_DIM
    ones_bd = jnp.asarray(blk[:, None] == blk[None, :], dtype=BF16)
    qg_t = jnp.tile(qg.astype(F32), MB_HEADS)[None, :]
    kg_t = jnp.tile(kg.astype(F32), MB_HEADS)[None, :]
    n_pos = seq // tm

    row = lambda n: pl.BlockSpec((tm, n), lambda i: (i, 0))
    full = lambda a: pl.BlockSpec(a.shape, lambda i: (0,) * a.ndim)
    outs = [jax.ShapeDtypeStruct((t, HG_WIDTH), BF16),
            jax.ShapeDtypeStruct((t, HG_WIDTH), F32),
            jax.ShapeDtypeStruct((t, HG_WIDTH), BF16),
            jax.ShapeDtypeStruct((t, HG_WIDTH), BF16),
            jax.ShapeDtypeStruct((t, HG_WIDTH), BF16),
            jax.ShapeDtypeStruct((t, MB_WIDTH), BF16),
            jax.ShapeDtypeStruct((t, MB_WIDTH), BF16),
            jax.ShapeDtypeStruct((t, MB_WIDTH), BF16),
            jax.ShapeDtypeStruct((t, d), BF16),
            jax.ShapeDtypeStruct((t, d), BF16)]
    est = (2 * tm * d * 4 + 2 * d * d_in * 2 + 2 * tm * (d_in * 2 + HG_WIDTH * 4)
           + 4 * tm * V7X_LANES * 4)
    return pl.pallas_call(
        functools.partial(_inproj_kernel, layer=layer),
        grid=(t // tm,),
        in_specs=[row(d), full(g1), full(w_in), full(lb_logits), full(qg_t), full(kg_t),
                  pl.BlockSpec((tm, V7X_LANES), lambda i: (i % n_pos, 0)),
                  pl.BlockSpec((tm, V7X_LANES), lambda i: (i % n_pos, 0)),
                  full(ones_bd)],
        out_specs=[row(o.shape[1]) for o in outs],
        out_shape=outs,
        compiler_params=pltpu.CompilerParams(dimension_semantics=("arbitrary",),
                                             vmem_limit_bytes=_vmem_limit(est)),
        name="inproj",
    )(x2, g1, w_in, lb_logits, qg_t, kg_t, rc, rs, ones_bd)


def _moba_kernel(q_ref, k_ref, v_ref, o_ref, vt_ref, km_ref, bias_ref):
    i = pl.program_id(2)
    nb, bk, _ = k_ref.shape
    bq = q_ref.shape[0]
    hd = MB_HEAD_DIM
    heads = V7X_LANES // hd

    @pl.when(i == 0)
    def _():
        for j in range(nb):
            vt_ref[j] = v_ref[j].astype(F32).T.astype(BF16)
            km_ref[pl.ds(j, 1), :] = jnp.mean(k_ref[j].astype(F32), axis=0, keepdims=True)

    q = q_ref[...]
    lane = lax.broadcasted_iota(jnp.int32, (1, V7X_LANES), 1)
    km = km_ref[...].astype(BF16)
    blk = lax.broadcasted_iota(jnp.int32, (nb, bq), 0)
    past = blk < i
    qs = []
    for h in range(heads):
        qh = jnp.where((lane // hd) == h, q, jnp.zeros_like(q))
        qs.append(qh)
        g = lax.dot_general(km, qh, _NT, preferred_element_type=F32)
        g = jnp.where(past, g, NEG_INF)
        cnt = jnp.zeros((nb, bq), F32)
        for jp in range(nb):
            gj = g[jp:jp + 1, :]
            beats = (gj > g) | ((gj == g) & (jp < blk))
            cnt = cnt + beats.astype(F32)
        sel = (cnt < float(MB_TOPK)) & past
        bias_ref[h] = jnp.where(sel, 0.0, NEG_INF)

    def scores(j, h):
        return lax.dot_general(k_ref[j], qs[h], _NT, preferred_element_type=F32)

    r = lax.broadcasted_iota(jnp.int32, (bk, bq), 0)
    c = lax.broadcasted_iota(jnp.int32, (bk, bq), 1)
    causal = r <= c
    carry = []
    for h in range(heads):
        s = jnp.where(causal, scores(i, h), NEG_INF)
        m = jnp.max(s, axis=0, keepdims=True)
        p = jnp.exp(s - m)
        l = jnp.sum(p, axis=0, keepdims=True)
        acc = jnp.dot(vt_ref[i, h * hd:(h + 1) * hd, :], p.astype(BF16), preferred_element_type=F32)
        carry += [m, l, acc]

    def body(j, carry):
        out = []
        for h in range(heads):
            m, l, acc = carry[3 * h:3 * h + 3]
            s = scores(j, h) + bias_ref[h, pl.ds(j, 1), :]
            mn = jnp.maximum(m, jnp.max(s, axis=0, keepdims=True))
            alpha = jnp.exp(m - mn)
            p = jnp.exp(s - mn)
            l = alpha * l + jnp.sum(p, axis=0, keepdims=True)
            acc = alpha * acc + jnp.dot(vt_ref[j, h * hd:(h + 1) * hd, :], p.astype(BF16),
                                        preferred_element_type=F32)
            out += [mn, l, acc]
        return tuple(out)

    carry = lax.fori_loop(0, i, body, tuple(carry))
    o_t = jnp.concatenate([carry[3 * h + 2] * (1.0 / carry[3 * h + 1]) for h in range(heads)], axis=0)
    o_ref[...] = o_t.T.astype(BF16)


def _moba(qb, kb, vb, batch, seq):
    nb = seq // MB_BLOCK
    pairs = MB_WIDTH // V7X_LANES
    q3 = qb.reshape(batch, seq, MB_WIDTH)
    k4 = kb.reshape(batch, nb, MB_BLOCK, MB_WIDTH)
    v4 = vb.reshape(batch, nb, MB_BLOCK, MB_WIDTH)
    kv_spec = pl.BlockSpec((None, nb, MB_BLOCK, V7X_LANES), lambda b, p, i: (b, 0, 0, p))
    q_spec = pl.BlockSpec((None, MB_BLOCK, V7X_LANES), lambda b, p, i: (b, i, p))
    est = (4 * seq * V7X_LANES * 2 + 4 * MB_BLOCK * V7X_LANES * 2 + seq * V7X_LANES * 2
           + 8 * MB_BLOCK * MB_BLOCK * 4)
    out = pl.pallas_call(
        _moba_kernel,
        grid=(batch, pairs, nb),
        in_specs=[q_spec, kv_spec, kv_spec],
        out_specs=q_spec,
        out_shape=jax.ShapeDtypeStruct((batch, seq, MB_WIDTH), BF16),
        scratch_shapes=[pltpu.VMEM((nb, V7X_LANES, MB_BLOCK), BF16),
                        pltpu.VMEM((nb, V7X_LANES), F32),
                        pltpu.VMEM((V7X_LANES // MB_HEAD_DIM, nb, MB_BLOCK), F32)],
        compiler_params=pltpu.CompilerParams(dimension_semantics=("arbitrary",) * 3,
                                             vmem_limit_bytes=_vmem_limit(est)),
        name="moba",
    )(q3, k4, v4)
    return out.reshape(batch * seq, MB_WIDTH)


def _hgrn_kernel(qa_ref, logf_ref, kk_ref, v_ref, gate_ref, og_ref, o_ref):
    seq = qa_ref.shape[0]
    ch = HG_CHUNK
    r = lax.broadcasted_iota(jnp.int32, (ch, ch), 0)
    c = lax.broadcasted_iota(jnp.int32, (ch, ch), 1)
    causal = r >= c
    tri = causal.astype(BF16)
    rows = lax.broadcasted_iota(jnp.int32, (ch, 1), 0)
    og = og_ref[...]

    def chunk(ci, st_t):
        r0 = pl.multiple_of(ci * ch, ch)
        lf = logf_ref[pl.ds(r0, ch), :]
        hi = lf.astype(BF16)
        lo = (lf - hi.astype(F32)).astype(BF16)
        cum = (jnp.dot(tri, hi, preferred_element_type=F32)
               + jnp.dot(tri, lo, preferred_element_type=F32))
        qa = qa_ref[pl.ds(r0, ch), :].astype(F32)
        kk = kk_ref[pl.ds(r0, ch), :].astype(F32)
        v = v_ref[pl.ds(r0, ch), :]
        last = cum[ch - 1:ch, :]
        qc = (qa * jnp.exp(cum)).astype(BF16)
        kbar = (kk * jnp.exp(last - cum)).astype(BF16)
        a_rows = []
        for a in range(ch // HG_SUB):
            lo_r, hi_r = a * HG_SUB, (a + 1) * HG_SUB
            rho = cum[lo_r + HG_SUB // 2 - 1:lo_r + HG_SUB // 2, :]
            qt = (qa[lo_r:hi_r] * jnp.exp(cum[lo_r:hi_r] - rho)).astype(BF16)
            kt = (kk * jnp.exp(jnp.where(rows < hi_r, rho - cum, NEG_INF))).astype(BF16)
            a_rows.append(lax.dot_general(qt, kt, _NT, preferred_element_type=F32))
        attn = jnp.where(causal, jnp.concatenate(a_rows, axis=0), 0.0).astype(BF16)
        o = (jnp.dot(attn, v, preferred_element_type=F32)
             + lax.dot_general(qc, st_t.astype(BF16), _NT, preferred_element_type=F32))
        st_t = st_t * jnp.exp(last) + lax.dot_general(v, kbar, _TN, preferred_element_type=F32)
        ms = jnp.mean(o * o, axis=-1, keepdims=True)
        y = o * lax.rsqrt(ms + NORM_EPS) * og
        o_ref[pl.ds(r0, ch), :] = (y * gate_ref[pl.ds(r0, ch), :].astype(F32)).astype(BF16)
        return st_t

    lax.fori_loop(0, seq // ch, chunk, jnp.zeros((HG_DIM, HG_DIM), F32))


def _hgrn(qa, logf, kk, vh, gh, og, batch, seq):
    r3 = lambda a: a.reshape(batch, seq, HG_WIDTH)
    spec = pl.BlockSpec((None, seq, HG_DIM), lambda b, h: (b, 0, h))
    est = 2 * seq * HG_DIM * (2 * 5 + 4)
    out = pl.pallas_call(
        _hgrn_kernel,
        grid=(batch, HG_HEADS),
        in_specs=[spec, spec, spec, spec, spec, pl.BlockSpec((1, HG_DIM), lambda b, h: (0, h))],
        out_specs=spec,
        out_shape=jax.ShapeDtypeStruct((batch, seq, HG_WIDTH), BF16),
        compiler_params=pltpu.CompilerParams(dimension_semantics=("arbitrary",) * 2,
                                             vmem_limit_bytes=_vmem_limit(est)),
        name="hgrn",
    )(r3(qa), r3(logf), r3(kk), r3(vh), r3(gh), og.reshape(1, HG_WIDTH))
    return out.reshape(batch * seq, HG_WIDTH)


def _merge_kernel(x_ref, oa_ref, ob_ref, sa_ref, sb_ref, wa_ref, wb_ref, wo_ref, g2_ref, x1_ref, h2_ref):
    ma = jnp.dot(oa_ref[...], wa_ref[...], preferred_element_type=F32)
    mb = jnp.dot(ob_ref[...], wb_ref[...], preferred_element_type=F32)
    mix = sa_ref[...].astype(F32) * ma + sb_ref[...].astype(F32) * mb
    x1 = x_ref[...] + jnp.dot(mix.astype(BF16), wo_ref[...], preferred_element_type=F32)
    x1_ref[...] = x1
    ms = jnp.mean(x1 * x1, axis=-1, keepdims=True)
    h2_ref[...] = (x1 * lax.rsqrt(ms + NORM_EPS) * g2_ref[...]).astype(BF16)


def _merge(x2, oa, ob, sa, sb, w_a, w_b, w_out, g2):
    t, d = x2.shape
    tm = ROW_TILE
    row = lambda n: pl.BlockSpec((tm, n), lambda i: (i, 0))
    full = lambda a: pl.BlockSpec(a.shape, lambda i: (0,) * a.ndim)
    est = 2 * tm * d * (4 + 4 + 2 + 2 + 2) + 4 * tm * HG_WIDTH * 2 + 2 * 2 * (2 * HG_WIDTH * d + d * d)
    return pl.pallas_call(
        _merge_kernel,
        grid=(t // tm,),
        in_specs=[row(d), row(HG_WIDTH), row(MB_WIDTH), row(d), row(d),
                  full(w_a), full(w_b), full(w_out), full(g2)],
        out_specs=[row(d), row(d)],
        out_shape=[jax.ShapeDtypeStruct((t, d), F32), jax.ShapeDtypeStruct((t, d), BF16)],
        compiler_params=pltpu.CompilerParams(dimension_semantics=("arbitrary",),
                                             vmem_limit_bytes=_vmem_limit(est)),
        name="merge",
    )(x2, oa, ob, sa, sb, w_a, w_b, w_out, g2)


def _ffn_kernel(h2_ref, x1_ref, wu_ref, cw_ref, cb_ref, wd_ref, o_ref, ubuf_ref, tail_ref, g_ref):
    s = pl.program_id(1)
    tm = h2_ref.shape[0]
    pad = V7X_SUBLANES
    h2 = h2_ref[...]
    inv_sqrt2 = float(1.0 / np.sqrt(2.0))
    for c in range(D_FF // FF_TILE):
        cs = slice(c * FF_TILE, (c + 1) * FF_TILE)
        u = jnp.dot(h2, wu_ref[:, cs], preferred_element_type=F32)
        v = jnp.dot(h2, wu_ref[:, D_FF + c * FF_TILE:D_FF + (c + 1) * FF_TILE], preferred_element_type=F32)
        ubuf_ref[0:pad, :] = jnp.where(s == 0, 0.0, tail_ref[c])
        ubuf_ref[pad:pad + tm, :] = u
        tail_ref[c] = u[tm - pad:tm, :]
        conv = (cb_ref[:, cs]
                + ubuf_ref[pad - 2:pad - 2 + tm, :] * cw_ref[0:1, cs]
                + ubuf_ref[pad - 1:pad - 1 + tm, :] * cw_ref[1:2, cs]
                + u * cw_ref[2:3, cs])
        gelu = 0.5 * conv * (1.0 + lax.erf(conv * inv_sqrt2))
        g_ref[:, cs] = (gelu * v).astype(BF16)
    o_ref[...] = x1_ref[...] + jnp.dot(g_ref[...], wd_ref[...], preferred_element_type=F32)


def _ffn(h2, x1, w_up, conv_w, conv_b, w_down, batch, seq):
    t, d = x1.shape
    tm = ROW_TILE
    n_s = seq // tm
    row = lambda n: pl.BlockSpec((tm, n), lambda b, s: (b * n_s + s, 0))
    full = lambda a: pl.BlockSpec(a.shape, lambda b, s: (0,) * a.ndim)
    est = (2 * tm * d * (2 + 4 + 4) + 2 * 2 * (d * 2 * D_FF + D_FF * d)
           + (tm + 8) * FF_TILE * 4 + tm * D_FF * 2 + 8 * D_FF * 4)
    return pl.pallas_call(
        _ffn_kernel,
        grid=(batch, n_s),
        in_specs=[row(d), row(d), full(w_up), full(conv_w), full(conv_b), full(w_down)],
        out_specs=row(d),
        out_shape=jax.ShapeDtypeStruct((t, d), F32),
        scratch_shapes=[pltpu.VMEM((tm + V7X_SUBLANES, FF_TILE), F32),
                        pltpu.VMEM((D_FF // FF_TILE, V7X_SUBLANES, FF_TILE), F32),
                        pltpu.VMEM((tm, D_FF), BF16)],
        compiler_params=pltpu.CompilerParams(dimension_semantics=("arbitrary",) * 2,
                                             vmem_limit_bytes=_vmem_limit(est)),
        name="ffn",
    )(h2, x1, w_up, conv_w, conv_b, w_down)


def kernel(x, norm1_g, w_in, hg_lb_logits, hg_onorm_g, q_norm_g, k_norm_g, w_a, w_b, w_out,
           norm2_g, w_up, conv_w, conv_b, w_down):
    batch, seq, d = x.shape
    depth = w_in.shape[0]
    x2 = x.reshape(batch * seq, d)
    for l in range(depth):
        qa, logf, kk, vh, gh, qb, kb, vb, sa, sb = _inproj(
            x2, norm1_g[l][None, :], w_in[l].astype(BF16), hg_lb_logits,
            q_norm_g[l], k_norm_g[l], seq, l)
        oa = _hgrn(qa, logf, kk, vh, gh, hg_onorm_g[l], batch, seq)
        ob = _moba(qb, kb, vb, batch, seq)
        x1, h2 = _merge(x2, oa, ob, sa, sb, w_a[l].astype(BF16), w_b[l].astype(BF16),
                        w_out[l].astype(BF16), norm2_g[l][None, :])
        x2 = _ffn(h2, x1, w_up[l].astype(BF16), conv_w[l], conv_b[l][None, :],
                  w_down[l].astype(BF16), batch, seq)
    return x2.reshape(batch, seq, d)
```

```python
import functools

import numpy as np
import jax
import jax.numpy as jnp
from jax import lax
from jax.experimental import pallas as pl
from jax.experimental.pallas import tpu as pltpu

F32 = jnp.float32
BF16 = jnp.bfloat16

HG_HEADS = 4
HG_DIM = 128
HG_WIDTH = HG_HEADS * HG_DIM
HG_CHUNK = 64
MB_HEADS = 8
MB_HEAD_DIM = 64
MB_WIDTH = MB_HEADS * MB_HEAD_DIM
MB_BLOCK = 256
MB_TOPK = 3
ROPE_THETA = 10000.0
D_FF = 2816
CONV_WIDTH = 3
NORM_EPS = 1e-6

V7X_LANES = 128
V7X_SUBLANES = 8
V7X_VMEM_BYTES = 64 * 1024 * 1024

HG_SUB = 16
HG_GROUP = 16
ROW_TILE = 512
FF_TILE = 256
NEG_INF = float("-inf")

_NT = (((1,), (1,)), ((), ()))
_TN = (((0,), (0,)), ((), ()))


def _vmem_limit(nbytes):
    return int(min(nbytes * 3 // 2 + (4 << 20), V7X_VMEM_BYTES - (4 << 20)))


def _sigmoid(x):
    return 1.0 / (1.0 + jnp.exp(-x))


def _inproj_kernel(x_ref, g1_ref, w_ref, lbl_ref, qg_ref, kg_ref, rc_ref, rs_ref, ones_ref,
                   qa_ref, logf_ref, kk_ref, vh_ref, gh_ref, qb_ref, kb_ref, vb_ref, sa_ref, sb_ref,
                   *, layer):
    x = x_ref[...]
    ms = jnp.mean(x * x, axis=-1, keepdims=True)
    h = (x * lax.rsqrt(ms + NORM_EPS) * g1_ref[...]).astype(BF16)

    def proj(c0, width):
        return jnp.dot(h, w_ref[:, c0:c0 + width], preferred_element_type=F32)

    w = HG_WIDTH
    p = proj(0, w)
    qa_ref[...] = (p * _sigmoid(p)).astype(BF16)
    a = lbl_ref[...]
    amax = jnp.max(a, axis=0, keepdims=True)
    e = jnp.exp(a - amax)
    lb = jnp.sum(e[0:layer + 1, :], axis=0, keepdims=True) / jnp.sum(e, axis=0, keepdims=True)
    p = proj(w, w)
    f = lb + (1.0 - lb) * _sigmoid(p)
    logf_ref[...] = jnp.log(f)
    kk_ref[...] = (1.0 - f).astype(BF16)
    vh_ref[...] = proj(2 * w, w).astype(BF16)
    p = proj(3 * w, w)
    gh_ref[...] = (p * _sigmoid(p)).astype(BF16)

    lane = lax.broadcasted_iota(jnp.int32, (1, V7X_LANES), 1)
    first_half = (lane % MB_HEAD_DIM) < (MB_HEAD_DIM // 2)
    rc = rc_ref[...]
    rs = rs_ref[...]
    ones_bd = ones_ref[...]

    def norm_rope(p, g_ref, out_ref, scale):
        for c in range(MB_WIDTH // V7X_LANES):
            sl = slice(c * V7X_LANES, (c + 1) * V7X_LANES)
            pc = p[:, sl]
            ss = jnp.dot((pc * pc).astype(BF16), ones_bd, preferred_element_type=F32)
            y = pc * lax.rsqrt(ss * (1.0 / MB_HEAD_DIM) + NORM_EPS) * g_ref[:, sl]
            partner = jnp.where(first_half,
                                pltpu.roll(y, V7X_LANES - MB_HEAD_DIM // 2, 1),
                                pltpu.roll(y, MB_HEAD_DIM // 2, 1))
            out_ref[:, sl] = ((y * rc + partner * rs) * scale).astype(BF16)

    base = 4 * w
    norm_rope(proj(base, MB_WIDTH), qg_ref, qb_ref, 1.0 / float(np.sqrt(MB_HEAD_DIM)))
    norm_rope(proj(base + MB_WIDTH, MB_WIDTH), kg_ref, kb_ref, 1.0)
    vb_ref[...] = proj(base + 2 * MB_WIDTH, MB_WIDTH).astype(BF16)
    base = base + 3 * MB_WIDTH
    d = sa_ref.shape[1]
    sa_ref[...] = _sigmoid(proj(base, d)).astype(BF16)
    sb_ref[...] = _sigmoid(proj(base + d, d)).astype(BF16)


def _rope_tables(seq):
    half = MB_HEAD_DIM // 2
    inv = 1.0 / (ROPE_THETA ** (jnp.arange(half, dtype=F32) * 2.0 / MB_HEAD_DIM))
    ang = jnp.arange(seq).astype(F32)[:, None] * inv[None, :]
    cos = jnp.cos(ang)
    sin = jnp.sin(ang)
    rc = jnp.concatenate([cos, cos, cos, cos], axis=-1)
    rs = jnp.concatenate([-sin, sin, -sin, sin], axis=-1)
    return rc, rs


def _inproj(x2, g1, w_in, lb_logits, qg, kg, seq, layer):
    t, d = x2.shape
    d_in = w_in.shape[1]
    tm = ROW_TILE
    rc, rs = _rope_tables(seq)
    blk = np.arange(V7X_LANES) // MB_HEAD_DIM
    ones_bd = jnp.asarray(blk[:, None] == blk[None, :], dtype=BF16)
    qg_t = jnp.tile(qg.astype(F32), MB_HEADS)[None, :]
    kg_t = jnp.tile(kg.astype(F32), MB_HEADS)[None, :]
    n_pos = seq // tm

    row = lambda n: pl.BlockSpec((tm, n), lambda i: (i, 0))
    full = lambda a: pl.BlockSpec(a.shape, lambda i: (0,) * a.ndim)
    outs = [jax.ShapeDtypeStruct((t, HG_WIDTH), BF16),
            jax.ShapeDtypeStruct((t, HG_WIDTH), F32),
            jax.ShapeDtypeStruct((t, HG_WIDTH), BF16),
            jax.ShapeDtypeStruct((t, HG_WIDTH), BF16),
            jax.ShapeDtypeStruct((t, HG_WIDTH), BF16),
            jax.ShapeDtypeStruct((t, MB_WIDTH), BF16),
            jax.ShapeDtypeStruct((t, MB_WIDTH), BF16),
            jax.ShapeDtypeStruct((t, MB_WIDTH), BF16),
            jax.ShapeDtypeStruct((t, d), BF16),
            jax.ShapeDtypeStruct((t, d), BF16)]
    est = (2 * tm * d * 4 + 2 * d * d_in * 2 + 2 * tm * (d_in * 2 + HG_WIDTH * 4)
           + 4 * tm * V7X_LANES * 4)
    return pl.pallas_call(
        functools.partial(_inproj_kernel, layer=layer),
        grid=(t // tm,),
        in_specs=[row(d), full(g1), full(w_in), full(lb_logits), full(qg_t), full(kg_t),
                  pl.BlockSpec((tm, V7X_LANES), lambda i: (i % n_pos, 0)),
                  pl.BlockSpec((tm, V7X_LANES), lambda i: (i % n_pos, 0)),
                  full(ones_bd)],
        out_specs=[row(o.shape[1]) for o in outs],
        out_shape=outs,
        compiler_params=pltpu.CompilerParams(dimension_semantics=("arbitrary",),
                                             vmem_limit_bytes=_vmem_limit(est)),
        name="inproj",
    )(x2, g1, w_in, lb_logits, qg_t, kg_t, rc, rs, ones_bd)


def _moba_attend(qi, q_ref, k_ref, vt_ref, km_ref, o_ref):
    nb, bk, _ = k_ref.shape
    bq = q_ref.shape[1]
    hd = MB_HEAD_DIM
    q = q_ref[qi]
    lane = lax.broadcasted_iota(jnp.int32, (1, V7X_LANES), 1)
    r = lax.broadcasted_iota(jnp.int32, (bk, bq), 0)
    c = lax.broadcasted_iota(jnp.int32, (bk, bq), 1)
    causal = r <= c
    n_sel = max(1, min(MB_TOPK, nb - 1))
    outs = []
    for h in range(V7X_LANES // hd):
        qh = jnp.where((lane // hd) == h, q, jnp.zeros_like(q))
        bias = None
        if qi > n_sel:
            g = lax.dot_general(km_ref[...].astype(BF16), qh, _NT, preferred_element_type=F32)
            blk = lax.broadcasted_iota(jnp.int32, (nb, bq), 0)
            cnt = jnp.zeros((nb, bq), F32)
            for jp in range(qi):
                gj = g[jp:jp + 1, :]
                cnt = cnt + ((gj > g) | ((gj == g) & (jp < blk))).astype(F32)
            bias = jnp.where((cnt < float(n_sel)) & (blk < qi), 0.0, NEG_INF)
        s_list = []
        for j in range(qi):
            s = lax.dot_general(k_ref[j], qh, _NT, preferred_element_type=F32)
            s_list.append(s if bias is None else s + bias[j:j + 1, :])
        s_list.append(jnp.where(causal, lax.dot_general(k_ref[qi], qh, _NT, preferred_element_type=F32), NEG_INF))
        m = jnp.max(functools.reduce(jnp.maximum, s_list), axis=0, keepdims=True)
        p_list = [jnp.exp(s - m) for s in s_list]
        l = jnp.sum(functools.reduce(jnp.add, p_list), axis=0, keepdims=True)
        p_all = jnp.concatenate([p.astype(BF16) for p in p_list], axis=0)
        acc = jnp.dot(vt_ref[h * hd:(h + 1) * hd, 0:(qi + 1) * bk], p_all, preferred_element_type=F32)
        outs.append(acc * (1.0 / l))
    o_ref[qi] = jnp.concatenate(outs, axis=0).T.astype(BF16)


def _moba_kernel(q_ref, k_ref, v_ref, o_ref, vt_ref, km_ref):
    t = pl.program_id(2)
    nb, bk, _ = k_ref.shape

    @pl.when(t == 0)
    def _():
        for j in range(nb):
            vt_ref[:, j * bk:(j + 1) * bk] = v_ref[j].astype(F32).T.astype(BF16)
            km_ref[j:j + 1, :] = jnp.mean(k_ref[j].astype(F32), axis=0, keepdims=True)

    for tt in range(nb // 2):
        @pl.when(t == tt)
        def _(tt=tt):
            _moba_attend(tt, q_ref, k_ref, vt_ref, km_ref, o_ref)
            _moba_attend(nb - 1 - tt, q_ref, k_ref, vt_ref, km_ref, o_ref)


def _moba(qb, kb, vb, batch, seq):
    nb = seq // MB_BLOCK
    assert nb % 2 == 0
    pairs = MB_WIDTH // V7X_LANES
    r4 = lambda a: a.reshape(batch, nb, MB_BLOCK, MB_WIDTH)
    spec = pl.BlockSpec((None, nb, MB_BLOCK, V7X_LANES), lambda b, p, t: (b, 0, 0, p))
    est = 2 * 4 * seq * V7X_LANES * 2 + seq * V7X_LANES * 2 + 2 * seq * MB_BLOCK * 4
    out = pl.pallas_call(
        _moba_kernel,
        grid=(batch, pairs, nb // 2),
        in_specs=[spec, spec, spec],
        out_specs=spec,
        out_shape=jax.ShapeDtypeStruct((batch, nb, MB_BLOCK, MB_WIDTH), BF16),
        scratch_shapes=[pltpu.VMEM((V7X_LANES, seq), BF16),
                        pltpu.VMEM((nb, V7X_LANES), F32)],
        compiler_params=pltpu.CompilerParams(dimension_semantics=("arbitrary",) * 3,
                                             vmem_limit_bytes=_vmem_limit(est)),
        name="moba",
    )(r4(qb), r4(kb), r4(vb))
    return out.reshape(batch * seq, MB_WIDTH)


def _hgrn_kernel(qa_ref, logf_ref, kk_ref, v_ref, gate_ref, og_ref, o_ref):
    seq = qa_ref.shape[0]
    ch = HG_CHUNK
    tile = HG_GROUP * ch
    causal = (lax.broadcasted_iota(jnp.int32, (ch, ch), 0) >= lax.broadcasted_iota(jnp.int32, (ch, ch), 1))
    tri = causal.astype(BF16)
    rows = lax.broadcasted_iota(jnp.int32, (ch, 1), 0)
    og = og_ref[...]

    def step(ti, st_t):
        t0 = pl.multiple_of(ti * tile, tile)
        lf_t = logf_ref[pl.ds(t0, tile), :]
        hi_t = lf_t.astype(BF16)
        lo_t = (lf_t - hi_t.astype(F32)).astype(BF16)
        qa_t = qa_ref[pl.ds(t0, tile), :].astype(F32)
        kk_t = kk_ref[pl.ds(t0, tile), :].astype(F32)
        v_t = v_ref[pl.ds(t0, tile), :]
        gate_t = gate_ref[pl.ds(t0, tile), :].astype(F32)
        for g in range(HG_GROUP):
            cs = slice(g * ch, (g + 1) * ch)
            cum = (jnp.dot(tri, hi_t[cs], preferred_element_type=F32)
                   + jnp.dot(tri, lo_t[cs], preferred_element_type=F32))
            qa, kk, v = qa_t[cs], kk_t[cs], v_t[cs]
            last = cum[ch - 1:ch, :]
            qc = (qa * jnp.exp(cum)).astype(BF16)
            kbar = (kk * jnp.exp(last - cum)).astype(BF16)
            a_rows = []
            for a in range(ch // HG_SUB):
                lo_r, hi_r = a * HG_SUB, (a + 1) * HG_SUB
                rho = cum[lo_r + HG_SUB // 2 - 1:lo_r + HG_SUB // 2, :]
                qt = (qa[lo_r:hi_r] * jnp.exp(cum[lo_r:hi_r] - rho)).astype(BF16)
                kt = (kk * jnp.exp(jnp.where(rows < hi_r, rho - cum, NEG_INF))).astype(BF16)
                a_rows.append(lax.dot_general(qt, kt, _NT, preferred_element_type=F32))
            attn = jnp.where(causal, jnp.concatenate(a_rows, axis=0), 0.0).astype(BF16)
            o = (jnp.dot(attn, v, preferred_element_type=F32)
                 + lax.dot_general(qc, st_t.astype(BF16), _NT, preferred_element_type=F32))
            st_t = st_t * jnp.exp(last) + lax.dot_general(v, kbar, _TN, preferred_element_type=F32)
            ms = jnp.mean(o * o, axis=-1, keepdims=True)
            y = o * lax.rsqrt(ms + NORM_EPS) * og
            o_ref[pl.ds(t0 + g * ch, ch), :] = (y * gate_t[cs]).astype(BF16)
        return st_t

    lax.fori_loop(0, seq // tile, step, jnp.zeros((HG_DIM, HG_DIM), F32))


def _hgrn(qa, logf, kk, vh, gh, og, batch, seq):
    assert seq % (HG_GROUP * HG_CHUNK) == 0
    r3 = lambda a: a.reshape(batch, seq, HG_WIDTH)
    spec = pl.BlockSpec((None, seq, HG_DIM), lambda b, h: (b, 0, h))
    est = 2 * seq * HG_DIM * (2 * 5 + 4)
    out = pl.pallas_call(
        _hgrn_kernel,
        grid=(batch, HG_HEADS),
        in_specs=[spec, spec, spec, spec, spec, pl.BlockSpec((1, HG_DIM), lambda b, h: (0, h))],
        out_specs=spec,
        out_shape=jax.ShapeDtypeStruct((batch, seq, HG_WIDTH), BF16),
        compiler_params=pltpu.CompilerParams(dimension_semantics=("arbitrary",) * 2,
                                             vmem_limit_bytes=_vmem_limit(est)),
        name="hgrn",
    )(r3(qa), r3(logf), r3(kk), r3(vh), r3(gh), og.reshape(1, HG_WIDTH))
    return out.reshape(batch * seq, HG_WIDTH)


def _merge_kernel(x_ref, oa_ref, ob_ref, sa_ref, sb_ref, wa_ref, wb_ref, wo_ref, g2_ref, x1_ref, h2_ref):
    ma = jnp.dot(oa_ref[...], wa_ref[...], preferred_element_type=F32)
    mb = jnp.dot(ob_ref[...], wb_ref[...], preferred_element_type=F32)
    mix = sa_ref[...].astype(F32) * ma + sb_ref[...].astype(F32) * mb
    x1 = x_ref[...] + jnp.dot(mix.astype(BF16), wo_ref[...], preferred_element_type=F32)
    x1_ref[...] = x1
    ms = jnp.mean(x1 * x1, axis=-1, keepdims=True)
    h2_ref[...] = (x1 * lax.rsqrt(ms + NORM_EPS) * g2_ref[...]).astype(BF16)


def _merge(x2, oa, ob, sa, sb, w_a, w_b, w_out, g2):
    t, d = x2.shape
    tm = ROW_TILE
    row = lambda n: pl.BlockSpec((tm, n), lambda i: (i, 0))
    full = lambda a: pl.BlockSpec(a.shape, lambda i: (0,) * a.ndim)
    est = 2 * tm * d * (4 + 4 + 2 + 2 + 2) + 4 * tm * HG_WIDTH * 2 + 2 * 2 * (2 * HG_WIDTH * d + d * d)
    return pl.pallas_call(
        _merge_kernel,
        grid=(t // tm,),
        in_specs=[row(d), row(HG_WIDTH), row(MB_WIDTH), row(d), row(d),
                  full(w_a), full(w_b), full(w_out), full(g2)],
        out_specs=[row(d), row(d)],
        out_shape=[jax.ShapeDtypeStruct((t, d), F32), jax.ShapeDtypeStruct((t, d), BF16)],
        compiler_params=pltpu.CompilerParams(dimension_semantics=("arbitrary",),
                                             vmem_limit_bytes=_vmem_limit(est)),
        name="merge",
    )(x2, oa, ob, sa, sb, w_a, w_b, w_out, g2)


def _ffn_kernel(h2_ref, x1_ref, wu_ref, cw_ref, cb_ref, wd_ref, o_ref, ubuf_ref, tail_ref, g_ref):
    tm = h2_ref.shape[0]
    pad = V7X_SUBLANES

    @pl.when(pl.program_id(1) == 0)
    def _():
        tail_ref[...] = jnp.zeros_like(tail_ref)

    h2 = h2_ref[...]
    inv_sqrt2 = float(1.0 / np.sqrt(2.0))
    for c in range(D_FF // FF_TILE):
        cs = slice(c * FF_TILE, (c + 1) * FF_TILE)
        u = jnp.dot(h2, wu_ref[:, cs], preferred_element_type=F32)
        v = jnp.dot(h2, wu_ref[:, D_FF + c * FF_TILE:D_FF + (c + 1) * FF_TILE], preferred_element_type=F32)
        ubuf_ref[0:pad, :] = tail_ref[c]
        ubuf_ref[pad:pad + tm, :] = u
        tail_ref[c] = u[tm - pad:tm, :]
        conv = (cb_ref[:, cs]
                + ubuf_ref[pad - 2:pad - 2 + tm, :] * cw_ref[0:1, cs]
                + ubuf_ref[pad - 1:pad - 1 + tm, :] * cw_ref[1:2, cs]
                + u * cw_ref[2:3, cs])
        gelu = 0.5 * conv * (1.0 + lax.erf(conv * inv_sqrt2))
        g_ref[:, cs] = (gelu * v).astype(BF16)
    o_ref[...] = x1_ref[...] + jnp.dot(g_ref[...], wd_ref[...], preferred_element_type=F32)


def _ffn(h2, x1, w_up, conv_w, conv_b, w_down, batch, seq):
    t, d = x1.shape
    tm = ROW_TILE
    n_s = seq // tm
    row = lambda n: pl.BlockSpec((tm, n), lambda b, s: (b * n_s + s, 0))
    full = lambda a: pl.BlockSpec(a.shape, lambda b, s: (0,) * a.ndim)
    est = (2 * tm * d * (2 + 4 + 4) + 2 * 2 * (d * 2 * D_FF + D_FF * d)
           + (tm + 8) * FF_TILE * 4 + tm * D_FF * 2 + 8 * D_FF * 4)
    return pl.pallas_call(
        _ffn_kernel,
        grid=(batch, n_s),
        in_specs=[row(d), row(d), full(w_up), full(conv_w), full(conv_b), full(w_down)],
        out_specs=row(d),
        out_shape=jax.ShapeDtypeStruct((t, d), F32),
        scratch_shapes=[pltpu.VMEM((tm + V7X_SUBLANES, FF_TILE), F32),
                        pltpu.VMEM((D_FF // FF_TILE, V7X_SUBLANES, FF_TILE), F32),
                        pltpu.VMEM((tm, D_FF), BF16)],
        compiler_params=pltpu.CompilerParams(dimension_semantics=("arbitrary",) * 2,
                                             vmem_limit_bytes=_vmem_limit(est)),
        name="ffn",
    )(h2, x1, w_up, conv_w, conv_b, w_down)


def kernel(x, norm1_g, w_in, hg_lb_logits, hg_onorm_g, q_norm_g, k_norm_g, w_a, w_b, w_out,
           norm2_g, w_up, conv_w, conv_b, w_down):
    batch, seq, d = x.shape
    depth = w_in.shape[0]
    x2 = x.reshape(batch * seq, d)
    for l in range(depth):
        qa, logf, kk, vh, gh, qb, kb, vb, sa, sb = _inproj(
            x2, norm1_g[l][None, :], w_in[l].astype(BF16), hg_lb_logits,
            q_norm_g[l], k_norm_g[l], seq, l)
        oa = _hgrn(qa, logf, kk, vh, gh, hg_onorm_g[l], batch, seq)
        ob = _moba(qb, kb, vb, batch, seq)
        x1, h2 = _merge(x2, oa, ob, sa, sb, w_a[l].astype(BF16), w_b[l].astype(BF16),
                        w_out[l].astype(BF16), norm2_g[l][None, :])
        x2 = _ffn(h2, x1, w_up[l].astype(BF16), conv_w[l], conv_b[l][None, :],
                  w_down[l].astype(BF16), batch, seq)
    return x2.reshape(batch, seq, d)
```

```python
import functools

import numpy as np
import jax
import jax.numpy as jnp
from jax import lax
from jax.experimental import pallas as pl
from jax.experimental.pallas import tpu as pltpu

F32 = jnp.float32
BF16 = jnp.bfloat16

HG_HEADS = 4
HG_DIM = 128
HG_WIDTH = HG_HEADS * HG_DIM
HG_CHUNK = 64
MB_HEADS = 8
MB_HEAD_DIM = 64
MB_WIDTH = MB_HEADS * MB_HEAD_DIM
MB_BLOCK = 256
MB_TOPK = 3
ROPE_THETA = 10000.0
D_FF = 2816
CONV_WIDTH = 3
NORM_EPS = 1e-6

V7X_LANES = 128
V7X_SUBLANES = 8
V7X_BF16_ROWS = 16
V7X_VMEM_BYTES = 64 * 1024 * 1024

HG_SUB = 16
HG_GROUP = 32
ROW_TILE = 512
FF_TILE = 256
NEG_INF = float("-inf")

MB_GROUPS = MB_WIDTH // V7X_LANES
MB_VROWS = MB_HEAD_DIM + V7X_BF16_ROWS

_NT = (((1,), (1,)), ((), ()))
_TN = (((0,), (0,)), ((), ()))


def _vmem_limit(nbytes):
    return int(min(nbytes * 3 // 2 + (4 << 20), V7X_VMEM_BYTES - (4 << 20)))


def _sigmoid(x):
    return 0.5 * jnp.tanh(0.5 * x) + 0.5


def _rms_norm(x, g):
    ms = jnp.mean(x * x, axis=-1, keepdims=True)
    return x * lax.rsqrt(ms + NORM_EPS) * g


def _store_groups(ref, val):
    for gi in range(ref.shape[0]):
        ref[gi] = val[:, gi * V7X_LANES:(gi + 1) * V7X_LANES].astype(ref.dtype)


def _load_groups(ref):
    return jnp.concatenate([ref[gi] for gi in range(ref.shape[0])], axis=1)


def _inproj_kernel(x_ref, g1_ref, w_ref, lbl_ref, qg_ref, kg_ref, rc_ref, rs_ref, ones_ref,
                   qa_ref, logf_ref, kk_ref, vh_ref, gh_ref, qb_ref, kb_ref, vb_ref, *, layer):
    h = _rms_norm(x_ref[...], g1_ref[...]).astype(BF16)

    def proj(c0, width):
        return jnp.dot(h, w_ref[:, c0:c0 + width], preferred_element_type=F32)

    w = HG_WIDTH
    p = proj(0, w)
    _store_groups(qa_ref, p * _sigmoid(p))
    a = lbl_ref[...]
    e = jnp.exp(a - jnp.max(a, axis=0, keepdims=True))
    lb = jnp.sum(e[0:layer + 1, :], axis=0, keepdims=True) / jnp.sum(e, axis=0, keepdims=True)
    f = lb + (1.0 - lb) * _sigmoid(proj(w, w))
    _store_groups(logf_ref, jnp.log(f))
    _store_groups(kk_ref, 1.0 - f)
    _store_groups(vh_ref, proj(2 * w, w))
    p = proj(3 * w, w)
    _store_groups(gh_ref, p * _sigmoid(p))

    lane = lax.broadcasted_iota(jnp.int32, (1, V7X_LANES), 1)
    first_half = (lane % MB_HEAD_DIM) < (MB_HEAD_DIM // 2)
    rc = rc_ref[...]
    rs = rs_ref[...]
    ones_bd = ones_ref[...]

    def norm_rope(p, g_ref, out_ref, scale):
        for c in range(MB_GROUPS):
            sl = slice(c * V7X_LANES, (c + 1) * V7X_LANES)
            pc = p[:, sl]
            ss = jnp.dot((pc * pc).astype(BF16), ones_bd, preferred_element_type=F32)
            y = pc * lax.rsqrt(ss * (1.0 / MB_HEAD_DIM) + NORM_EPS) * g_ref[:, sl]
            partner = jnp.where(first_half,
                                pltpu.roll(y, V7X_LANES - MB_HEAD_DIM // 2, 1),
                                pltpu.roll(y, MB_HEAD_DIM // 2, 1))
            out_ref[c] = ((y * rc + partner * rs) * scale).astype(BF16)

    base = 4 * w
    norm_rope(proj(base, MB_WIDTH), qg_ref, qb_ref, 1.0 / float(np.sqrt(MB_HEAD_DIM)))
    norm_rope(proj(base + MB_WIDTH, MB_WIDTH), kg_ref, kb_ref, 1.0)
    _store_groups(vb_ref, proj(base + 2 * MB_WIDTH, MB_WIDTH))


def _rope_tables(seq):
    half = MB_HEAD_DIM // 2
    inv = 1.0 / (ROPE_THETA ** (jnp.arange(half, dtype=F32) * 2.0 / MB_HEAD_DIM))
    ang = jnp.arange(seq).astype(F32)[:, None] * inv[None, :]
    cos = jnp.cos(ang)
    sin = jnp.sin(ang)
    rc = jnp.concatenate([cos, cos, cos, cos], axis=-1)
    rs = jnp.concatenate([-sin, sin, -sin, sin], axis=-1)
    return rc, rs


def _inproj(x2, g1, w_mix, lb_logits, qg, kg, seq, layer):
    t, d = x2.shape
    d_in = w_mix.shape[1]
    tm = ROW_TILE
    rc, rs = _rope_tables(seq)
    blk = np.arange(V7X_LANES) // MB_HEAD_DIM
    ones_bd = jnp.asarray(blk[:, None] == blk[None, :], dtype=BF16)
    qg_t = jnp.tile(qg.astype(F32), MB_HEADS)[None, :]
    kg_t = jnp.tile(kg.astype(F32), MB_HEADS)[None, :]
    n_pos = seq // tm

    full = lambda a: pl.BlockSpec(a.shape, lambda i: (0,) * a.ndim)
    grouped = lambda n, dt: jax.ShapeDtypeStruct((n // V7X_LANES, t, V7X_LANES), dt)
    outs = [grouped(HG_WIDTH, BF16),
            grouped(HG_WIDTH, F32),
            grouped(HG_WIDTH, BF16),
            grouped(HG_WIDTH, BF16),
            grouped(HG_WIDTH, BF16),
            grouped(MB_WIDTH, BF16),
            grouped(MB_WIDTH, BF16),
            grouped(MB_WIDTH, BF16)]
    est = (2 * tm * d * 4 + 2 * d * d_in * 2 + 2 * tm * (d_in * 2 + HG_WIDTH * 4)
           + 4 * tm * V7X_LANES * 4)
    return pl.pallas_call(
        functools.partial(_inproj_kernel, layer=layer),
        grid=(t // tm,),
        in_specs=[pl.BlockSpec((tm, d), lambda i: (i, 0)),
                  full(g1), full(w_mix), full(lb_logits), full(qg_t), full(kg_t),
                  pl.BlockSpec((tm, V7X_LANES), lambda i: (i % n_pos, 0)),
                  pl.BlockSpec((tm, V7X_LANES), lambda i: (i % n_pos, 0)),
                  full(ones_bd)],
        out_specs=[pl.BlockSpec((o.shape[0], tm, V7X_LANES), lambda i: (0, i, 0)) for o in outs],
        out_shape=outs,
        compiler_params=pltpu.CompilerParams(dimension_semantics=("arbitrary",),
                                             vmem_limit_bytes=_vmem_limit(est)),
        name="inproj",
    )(x2, g1, w_mix, lb_logits, qg_t, kg_t, rc, rs, ones_bd)


def _moba_kernel(q_ref, k_ref, v_ref, o_ref, vt_ref, km_ref):
    nb, bk, _ = k_ref.shape
    bq = q_ref.shape[1]
    hd = MB_HEAD_DIM
    heads = V7X_LANES // hd
    n_sel = max(1, min(MB_TOPK, nb - 1))

    ones = jnp.ones((V7X_BF16_ROWS, bk), BF16)
    for j in range(nb):
        v_t = v_ref[j].astype(F32).T.astype(BF16)
        for h in range(heads):
            vt_ref[h * MB_VROWS:h * MB_VROWS + hd, j * bk:(j + 1) * bk] = v_t[h * hd:(h + 1) * hd, :]
            vt_ref[h * MB_VROWS + hd:(h + 1) * MB_VROWS, j * bk:(j + 1) * bk] = ones
        km_ref[j:j + 1, :] = jnp.mean(k_ref[j].astype(F32), axis=0, keepdims=True)

    lane = lax.broadcasted_iota(jnp.int32, (1, V7X_LANES), 1)
    causal = (lax.broadcasted_iota(jnp.int32, (bk, bq), 0) <= lax.broadcasted_iota(jnp.int32, (bk, bq), 1))
    blk = lax.broadcasted_iota(jnp.int32, (nb, bq), 0)
    half = nb // 2
    q_order = list(range(half)) + list(range(nb - 1, half - 1, -1))
    units = [(qi, h) for qi in q_order for h in range(heads)]

    qh_, bias_ = {}, {}
    for (qi, h) in units:
        q = q_ref[qi]
        qh = jnp.where((lane // hd) == h, q, jnp.zeros_like(q))
        qh_[qi, h] = qh
        bias_[qi, h] = None
        if qi > n_sel:
            g = lax.dot_general(km_ref[...].astype(BF16), qh, _NT, preferred_element_type=F32)
            cnt = jnp.zeros((nb, bq), F32)
            for jp in range(qi):
                gj = g[jp:jp + 1, :]
                cnt = cnt + ((gj > g) | ((gj == g) & (jp < blk))).astype(F32)
            bias_[qi, h] = jnp.where((cnt < float(n_sel)) & (blk < qi), 0.0, NEG_INF)
    s_ = {}
    for (qi, h) in units:
        qh, bias = qh_[qi, h], bias_[qi, h]
        s_list = []
        for j in range(qi):
            s = lax.dot_general(k_ref[j], qh, _NT, preferred_element_type=F32)
            s_list.append(s if bias is None else s + bias[j:j + 1, :])
        s_list.append(jnp.where(causal, lax.dot_general(k_ref[qi], qh, _NT, preferred_element_type=F32), NEG_INF))
        s_[qi, h] = s_list
    m_ = {u: jnp.max(functools.reduce(jnp.maximum, s_[u]), axis=0, keepdims=True) for u in units}
    p_ = {u: jnp.concatenate([jnp.exp(s - m_[u]).astype(BF16) for s in s_[u]], axis=0) for u in units}
    acc_ = {}
    for (qi, h) in units:
        acc_[qi, h] = jnp.dot(vt_ref[h * MB_VROWS:(h + 1) * MB_VROWS, 0:(qi + 1) * bk], p_[qi, h],
                              preferred_element_type=F32)
    for qi in q_order:
        outs = [acc_[qi, h][0:hd, :] * (1.0 / acc_[qi, h][hd:hd + 1, :]) for h in range(heads)]
        o_ref[qi] = jnp.concatenate(outs, axis=0).T.astype(BF16)


def _moba(qb, kb, vb, batch, seq):
    nb = seq // MB_BLOCK
    assert nb % 2 == 0
    r5 = lambda a: a.reshape(MB_GROUPS, batch, nb, MB_BLOCK, V7X_LANES)
    spec = pl.BlockSpec((None, None, nb, MB_BLOCK, V7X_LANES), lambda b, p: (p, b, 0, 0, 0))
    est = 2 * 4 * seq * V7X_LANES * 2 + 2 * MB_VROWS * seq * 2 + 8 * seq * MB_BLOCK * 4
    out = pl.pallas_call(
        _moba_kernel,
        grid=(batch, MB_GROUPS),
        in_specs=[spec, spec, spec],
        out_specs=spec,
        out_shape=jax.ShapeDtypeStruct((MB_GROUPS, batch, nb, MB_BLOCK, V7X_LANES), BF16),
        scratch_shapes=[pltpu.VMEM((2 * MB_VROWS, seq), BF16),
                        pltpu.VMEM((nb, V7X_LANES), F32)],
        compiler_params=pltpu.CompilerParams(dimension_semantics=("arbitrary",) * 2,
                                             vmem_limit_bytes=_vmem_limit(est)),
        name="moba",
    )(r5(qb), r5(kb), r5(vb))
    return out.reshape(MB_GROUPS, batch * seq, V7X_LANES)


def _hgrn_kernel(qa_ref, logf_ref, kk_ref, v_ref, gate_ref, og_ref, o_ref):
    seq = qa_ref.shape[0]
    ch = HG_CHUNK
    tile = HG_GROUP * ch
    causal = (lax.broadcasted_iota(jnp.int32, (ch, ch), 0) >= lax.broadcasted_iota(jnp.int32, (ch, ch), 1))
    tri = causal.astype(BF16)
    rows = lax.broadcasted_iota(jnp.int32, (ch, 1), 0)
    og = og_ref[...]

    def step(ti, st_t):
        t0 = pl.multiple_of(ti * tile, tile)
        lf_t = logf_ref[pl.ds(t0, tile), :]
        hi_t = lf_t.astype(BF16)
        lo_t = (lf_t - hi_t.astype(F32)).astype(BF16)
        qa_t = qa_ref[pl.ds(t0, tile), :].astype(F32)
        kk_t = kk_ref[pl.ds(t0, tile), :].astype(F32)
        v_t = v_ref[pl.ds(t0, tile), :]
        gate_t = gate_ref[pl.ds(t0, tile), :].astype(F32)
        chunks = [slice(g * ch, (g + 1) * ch) for g in range(HG_GROUP)]
        cums = [jnp.dot(tri, hi_t[cs], preferred_element_type=F32)
                + jnp.dot(tri, lo_t[cs], preferred_element_type=F32) for cs in chunks]
        ops = []
        for cs, cum in zip(chunks, cums):
            qa, kk = qa_t[cs], kk_t[cs]
            last = cum[ch - 1:ch, :]
            qc = (qa * jnp.exp(cum)).astype(BF16)
            kbar = (kk * jnp.exp(last - cum)).astype(BF16)
            qts, kts = [], []
            for a in range(ch // HG_SUB):
                lo_r, hi_r = a * HG_SUB, (a + 1) * HG_SUB
                rho = cum[lo_r + HG_SUB // 2 - 1:lo_r + HG_SUB // 2, :]
                qts.append((qa[lo_r:hi_r] * jnp.exp(cum[lo_r:hi_r] - rho)).astype(BF16))
                kts.append((kk * jnp.exp(jnp.where(rows < hi_r, rho - cum, NEG_INF))).astype(BF16))
            ops.append((qc, kbar, qts, kts, jnp.exp(last)))
        attns = []
        for qc, kbar, qts, kts, decay in ops:
            a_rows = [lax.dot_general(qt, kt, _NT, preferred_element_type=F32) for qt, kt in zip(qts, kts)]
            attns.append(jnp.where(causal, jnp.concatenate(a_rows, axis=0), 0.0).astype(BF16))
        o_intra = [jnp.dot(attn, v_t[cs], preferred_element_type=F32) for attn, cs in zip(attns, chunks)]
        incs = [lax.dot_general(v_t[cs], op[1], _TN, preferred_element_type=F32) for op, cs in zip(ops, chunks)]
        for g, cs in enumerate(chunks):
            qc, decay = ops[g][0], ops[g][4]
            o = o_intra[g] + lax.dot_general(qc, st_t.astype(BF16), _NT, preferred_element_type=F32)
            st_t = st_t * decay + incs[g]
            o_ref[pl.ds(t0 + g * ch, ch), :] = (_rms_norm(o, og) * gate_t[cs]).astype(BF16)
        return st_t

    lax.fori_loop(0, seq // tile, step, jnp.zeros((HG_DIM, HG_DIM), F32))


def _hgrn(qa, logf, kk, vh, gh, og, batch, seq):
    assert seq % (HG_GROUP * HG_CHUNK) == 0
    r4 = lambda a: a.reshape(HG_HEADS, batch, seq, HG_DIM)
    spec = pl.BlockSpec((None, None, seq, HG_DIM), lambda b, h: (h, b, 0, 0))
    est = 2 * seq * HG_DIM * (2 * 5 + 4)
    out = pl.pallas_call(
        _hgrn_kernel,
        grid=(batch, HG_HEADS),
        in_specs=[spec, spec, spec, spec, spec, pl.BlockSpec((1, HG_DIM), lambda b, h: (0, h))],
        out_specs=spec,
        out_shape=jax.ShapeDtypeStruct((HG_HEADS, batch, seq, HG_DIM), BF16),
        compiler_params=pltpu.CompilerParams(dimension_semantics=("arbitrary",) * 2,
                                             vmem_limit_bytes=_vmem_limit(est)),
        name="hgrn",
    )(r4(qa), r4(logf), r4(kk), r4(vh), r4(gh), og.reshape(1, HG_WIDTH))
    return out.reshape(HG_HEADS, batch * seq, HG_DIM)


def _merge_kernel(x_ref, oa_ref, ob_ref, g1_ref, wg_ref, wa_ref, wb_ref, wo_ref, g2_ref, x1_ref, h2_ref):
    x = x_ref[...]
    d = x.shape[1]
    h1 = _rms_norm(x, g1_ref[...]).astype(BF16)
    ga = jnp.dot(h1, wg_ref[:, 0:d], preferred_element_type=F32)
    gb = jnp.dot(h1, wg_ref[:, d:2 * d], preferred_element_type=F32)
    ma = jnp.dot(_load_groups(oa_ref), wa_ref[...], preferred_element_type=F32)
    mb = jnp.dot(_load_groups(ob_ref), wb_ref[...], preferred_element_type=F32)
    mix = _sigmoid(ga) * ma + _sigmoid(gb) * mb
    x1 = x + jnp.dot(mix.astype(BF16), wo_ref[...], preferred_element_type=F32)
    x1_ref[...] = x1
    h2_ref[...] = _rms_norm(x1, g2_ref[...]).astype(BF16)


def _merge(x2, oa, ob, g1, w_gate, w_a, w_b, w_out, g2):
    t, d = x2.shape
    tm = ROW_TILE
    row = lambda n: pl.BlockSpec((tm, n), lambda i: (i, 0))
    grouped = lambda a: pl.BlockSpec((a.shape[0], tm, V7X_LANES), lambda i: (0, i, 0))
    full = lambda a: pl.BlockSpec(a.shape, lambda i: (0,) * a.ndim)
    est = (2 * tm * d * (4 + 4 + 2) + 4 * tm * HG_WIDTH * 2
           + 2 * 2 * (2 * d * d + 2 * HG_WIDTH * d + d * d))
    return pl.pallas_call(
        _merge_kernel,
        grid=(t // tm,),
        in_specs=[row(d), grouped(oa), grouped(ob), full(g1), full(w_gate),
                  full(w_a), full(w_b), full(w_out), full(g2)],
        out_specs=[row(d), row(d)],
        out_shape=[jax.ShapeDtypeStruct((t, d), F32), jax.ShapeDtypeStruct((t, d), BF16)],
        compiler_params=pltpu.CompilerParams(dimension_semantics=("arbitrary",),
                                             vmem_limit_bytes=_vmem_limit(est)),
        name="merge",
    )(x2, oa, ob, g1, w_gate, w_a, w_b, w_out, g2)


def _ffn_kernel(h2_ref, x1_ref, wu_ref, cw_ref, cb_ref, wd_ref, o_ref, ubuf_ref, tail_ref, g_ref):
    tm = h2_ref.shape[0]
    pad = V7X_SUBLANES

    @pl.when(pl.program_id(1) == 0)
    def _():
        tail_ref[...] = jnp.zeros_like(tail_ref)

    h2 = h2_ref[...]
    inv_sqrt2 = float(1.0 / np.sqrt(2.0))
    for c in range(D_FF // FF_TILE):
        cs = slice(c * FF_TILE, (c + 1) * FF_TILE)
        u = jnp.dot(h2, wu_ref[:, cs], preferred_element_type=F32)
        v = jnp.dot(h2, wu_ref[:, D_FF + c * FF_TILE:D_FF + (c + 1) * FF_TILE], preferred_element_type=F32)
        ubuf_ref[0:pad, :] = tail_ref[c]
        ubuf_ref[pad:pad + tm, :] = u
        tail_ref[c] = u[tm - pad:tm, :]
        conv = (cb_ref[:, cs]
                + ubuf_ref[pad - 2:pad - 2 + tm, :] * cw_ref[0:1, cs]
                + ubuf_ref[pad - 1:pad - 1 + tm, :] * cw_ref[1:2, cs]
                + u * cw_ref[2:3, cs])
        gelu = 0.5 * conv * (1.0 + lax.erf(conv * inv_sqrt2))
        g_ref[:, cs] = (gelu * v).astype(BF16)
    o_ref[...] = x1_ref[...] + jnp.dot(g_ref[...], wd_ref[...], preferred_element_type=F32)


def _ffn(h2, x1, w_up, conv_w, conv_b, w_down, batch, seq):
    t, d = x1.shape
    tm = ROW_TILE
    n_s = seq // tm
    row = lambda n: pl.BlockSpec((tm, n), lambda b, s: (b * n_s + s, 0))
    full = lambda a: pl.BlockSpec(a.shape, lambda b, s: (0,) * a.ndim)
    est = (2 * tm * d * (2 + 4 + 4) + 2 * 2 * (d * 2 * D_FF + D_FF * d)
           + (tm + 8) * FF_TILE * 4 + tm * D_FF * 2 + 8 * D_FF * 4)
    return pl.pallas_call(
        _ffn_kernel,
        grid=(batch, n_s),
        in_specs=[row(d), row(d), full(w_up), full(conv_w), full(conv_b), full(w_down)],
        out_specs=row(d),
        out_shape=jax.ShapeDtypeStruct((t, d), F32),
        scratch_shapes=[pltpu.VMEM((tm + V7X_SUBLANES, FF_TILE), F32),
                        pltpu.VMEM((D_FF // FF_TILE, V7X_SUBLANES, FF_TILE), F32),
                        pltpu.VMEM((tm, D_FF), BF16)],
        compiler_params=pltpu.CompilerParams(dimension_semantics=("arbitrary",) * 2,
                                             vmem_limit_bytes=_vmem_limit(est)),
        name="ffn",
    )(h2, x1, w_up, conv_w, conv_b, w_down)


def kernel(x, norm1_g, w_in, hg_lb_logits, hg_onorm_g, q_norm_g, k_norm_g, w_a, w_b, w_out,
           norm2_g, w_up, conv_w, conv_b, w_down):
    batch, seq, d = x.shape
    depth = w_in.shape[0]
    n_mix = 4 * HG_WIDTH + 3 * MB_WIDTH
    x2 = x.reshape(batch * seq, d)
    for l in range(depth):
        g1 = norm1_g[l][None, :]
        w_l = w_in[l].astype(BF16)
        qa, logf, kk, vh, gh, qb, kb, vb = _inproj(
            x2, g1, w_l[:, :n_mix], hg_lb_logits, q_norm_g[l], k_norm_g[l], seq, l)
        oa = _hgrn(qa, logf, kk, vh, gh, hg_onorm_g[l], batch, seq)
        ob = _moba(qb, kb, vb, batch, seq)
        x1, h2 = _merge(x2, oa, ob, g1, w_l[:, n_mix:], w_a[l].astype(BF16), w_b[l].astype(BF16),
                        w_out[l].astype(BF16), norm2_g[l][None, :])
        x2 = _ffn(h2, x1, w_up[l].astype(BF16), conv_w[l], conv_b[l][None, :],
                  w_down[l].astype(BF16), batch, seq)
    return x2.reshape(batch, seq, d)
```

```python
import functools

import numpy as np
import jax
import jax.numpy as jnp
from jax import lax
from jax.experimental import pallas as pl
from jax.experimental.pallas import tpu as pltpu

F32 = jnp.float32
BF16 = jnp.bfloat16

HG_HEADS = 4
HG_DIM = 128
HG_WIDTH = HG_HEADS * HG_DIM
HG_CHUNK = 64
MB_HEADS = 8
MB_HEAD_DIM = 64
MB_WIDTH = MB_HEADS * MB_HEAD_DIM
MB_BLOCK = 256
MB_TOPK = 3
ROPE_THETA = 10000.0
D_FF = 2816
CONV_WIDTH = 3
NORM_EPS = 1e-6

V7X_LANES = 128
V7X_SUBLANES = 8
V7X_BF16_ROWS = 16
V7X_VMEM_BYTES = 64 * 1024 * 1024

HG_SUB = 16
HG_GROUP = 32
ROW_TILE = 1024
FF_TILE = 256
NEG_INF = float("-inf")

MB_GROUPS = MB_WIDTH // V7X_LANES
MB_VROWS = MB_HEAD_DIM + V7X_BF16_ROWS

_NT = (((1,), (1,)), ((), ()))
_TN = (((0,), (0,)), ((), ()))


def _vmem_limit(nbytes):
    return int(min(nbytes * 3 // 2 + (4 << 20), V7X_VMEM_BYTES - (4 << 20)))


def _sigmoid(x):
    return 0.5 * jnp.tanh(0.5 * x) + 0.5


def _rms_norm(x, g):
    ms = jnp.mean(x * x, axis=-1, keepdims=True)
    return x * lax.rsqrt(ms + NORM_EPS) * g


def _resident(a):
    return pl.BlockSpec(a.shape, lambda *_: (0,) * a.ndim, pipeline_mode=pl.Buffered(1))


def _store_groups(ref, val):
    for gi in range(ref.shape[0]):
        ref[gi] = val[:, gi * V7X_LANES:(gi + 1) * V7X_LANES].astype(ref.dtype)


def _load_groups(ref):
    return jnp.concatenate([ref[gi] for gi in range(ref.shape[0])], axis=1)


def _inproj_kernel(x_ref, g1_ref, w_ref, lbl_ref, qg_ref, kg_ref, rc_ref, rs_ref, ones_ref,
                   qa_ref, logf_ref, kk_ref, vh_ref, gh_ref, qb_ref, kb_ref, vb_ref, *, layer):
    h = _rms_norm(x_ref[...], g1_ref[...]).astype(BF16)

    def proj(c0, width):
        return jnp.dot(h, w_ref[:, c0:c0 + width], preferred_element_type=F32)

    w = HG_WIDTH
    p = proj(0, w)
    _store_groups(qa_ref, p * _sigmoid(p))
    a = lbl_ref[...]
    e = jnp.exp(a - jnp.max(a, axis=0, keepdims=True))
    lb = jnp.sum(e[0:layer + 1, :], axis=0, keepdims=True) / jnp.sum(e, axis=0, keepdims=True)
    f = lb + (1.0 - lb) * _sigmoid(proj(w, w))
    _store_groups(logf_ref, jnp.log2(f))
    _store_groups(kk_ref, 1.0 - f)
    _store_groups(vh_ref, proj(2 * w, w))
    p = proj(3 * w, w)
    _store_groups(gh_ref, p * _sigmoid(p))

    lane = lax.broadcasted_iota(jnp.int32, (1, V7X_LANES), 1)
    first_half = (lane % MB_HEAD_DIM) < (MB_HEAD_DIM // 2)
    rc = rc_ref[...]
    rs = rs_ref[...]
    ones_bd = ones_ref[...]

    def norm_rope(p, g_ref, out_ref, scale):
        for c in range(MB_GROUPS):
            sl = slice(c * V7X_LANES, (c + 1) * V7X_LANES)
            pc = p[:, sl]
            ss = jnp.dot((pc * pc).astype(BF16), ones_bd, preferred_element_type=F32)
            y = pc * lax.rsqrt(ss * (1.0 / MB_HEAD_DIM) + NORM_EPS) * g_ref[:, sl]
            partner = jnp.where(first_half,
                                pltpu.roll(y, V7X_LANES - MB_HEAD_DIM // 2, 1),
                                pltpu.roll(y, MB_HEAD_DIM // 2, 1))
            out_ref[c] = ((y * rc + partner * rs) * scale).astype(BF16)

    base = 4 * w
    norm_rope(proj(base, MB_WIDTH), qg_ref, qb_ref, float(np.log2(np.e) / np.sqrt(MB_HEAD_DIM)))
    norm_rope(proj(base + MB_WIDTH, MB_WIDTH), kg_ref, kb_ref, 1.0)
    _store_groups(vb_ref, proj(base + 2 * MB_WIDTH, MB_WIDTH))


def _rope_tables(seq):
    half = MB_HEAD_DIM // 2
    inv = 1.0 / (ROPE_THETA ** (jnp.arange(half, dtype=F32) * 2.0 / MB_HEAD_DIM))
    ang = jnp.arange(seq).astype(F32)[:, None] * inv[None, :]
    cos = jnp.cos(ang)
    sin = jnp.sin(ang)
    rc = jnp.concatenate([cos, cos, cos, cos], axis=-1)
    rs = jnp.concatenate([-sin, sin, -sin, sin], axis=-1)
    return rc, rs


def _inproj(x2, g1, w_mix, lb_logits, qg, kg, seq, layer):
    t, d = x2.shape
    d_in = w_mix.shape[1]
    tm = ROW_TILE
    rc, rs = _rope_tables(seq)
    blk = np.arange(V7X_LANES) // MB_HEAD_DIM
    ones_bd = jnp.asarray(blk[:, None] == blk[None, :], dtype=BF16)
    qg_t = jnp.tile(qg.astype(F32), MB_HEADS)[None, :]
    kg_t = jnp.tile(kg.astype(F32), MB_HEADS)[None, :]
    n_pos = seq // tm

    full = _resident
    grouped = lambda n, dt: jax.ShapeDtypeStruct((n // V7X_LANES, t, V7X_LANES), dt)
    outs = [grouped(HG_WIDTH, BF16),
            grouped(HG_WIDTH, F32),
            grouped(HG_WIDTH, BF16),
            grouped(HG_WIDTH, BF16),
            grouped(HG_WIDTH, BF16),
            grouped(MB_WIDTH, BF16),
            grouped(MB_WIDTH, BF16),
            grouped(MB_WIDTH, BF16)]
    est = (2 * tm * d * 4 + d * d_in * 2 + 2 * tm * (d_in * 2 + HG_WIDTH * 4)
           + 4 * tm * V7X_LANES * 4 + 4 * tm * HG_WIDTH * 4)
    return pl.pallas_call(
        functools.partial(_inproj_kernel, layer=layer),
        grid=(t // tm,),
        in_specs=[pl.BlockSpec((tm, d), lambda i: (i, 0)),
                  full(g1), full(w_mix), full(lb_logits), full(qg_t), full(kg_t),
                  pl.BlockSpec((tm, V7X_LANES), lambda i: (i % n_pos, 0)),
                  pl.BlockSpec((tm, V7X_LANES), lambda i: (i % n_pos, 0)),
                  full(ones_bd)],
        out_specs=[pl.BlockSpec((o.shape[0], tm, V7X_LANES), lambda i: (0, i, 0)) for o in outs],
        out_shape=outs,
        compiler_params=pltpu.CompilerParams(dimension_semantics=("arbitrary",),
                                             vmem_limit_bytes=_vmem_limit(est)),
        name="inproj",
    )(x2, g1, w_mix, lb_logits, qg_t, kg_t, rc, rs, ones_bd)


def _moba_kernel(q_ref, k_ref, v_ref, o_ref, vt_ref, km_ref):
    nb, bk, _ = k_ref.shape
    bq = q_ref.shape[1]
    hd = MB_HEAD_DIM
    heads = V7X_LANES // hd
    n_sel = max(1, min(MB_TOPK, nb - 1))

    ones = jnp.ones((V7X_BF16_ROWS, bk), BF16)
    for j in range(nb):
        v_t = v_ref[j].astype(F32).T.astype(BF16)
        for h in range(heads):
            vt_ref[h * MB_VROWS:h * MB_VROWS + hd, j * bk:(j + 1) * bk] = v_t[h * hd:(h + 1) * hd, :]
            vt_ref[h * MB_VROWS + hd:(h + 1) * MB_VROWS, j * bk:(j + 1) * bk] = ones
        km_ref[j:j + 1, :] = jnp.mean(k_ref[j].astype(F32), axis=0, keepdims=True)

    lane = lax.broadcasted_iota(jnp.int32, (1, V7X_LANES), 1)
    causal = (lax.broadcasted_iota(jnp.int32, (bk, bq), 0) <= lax.broadcasted_iota(jnp.int32, (bk, bq), 1))
    blk = lax.broadcasted_iota(jnp.int32, (nb, bq), 0)
    live = {}

    def scores(qi):
        q = q_ref[qi]
        for h in range(heads):
            qh = jnp.where((lane // hd) == h, q, jnp.zeros_like(q))
            bias = None
            if qi > n_sel:
                g = lax.dot_general(km_ref[...].astype(BF16), qh, _NT, preferred_element_type=F32)
                cnt = jnp.zeros((nb, bq), F32)
                for jp in range(qi):
                    gj = g[jp:jp + 1, :]
                    cnt = cnt + ((gj > g) | ((gj == g) & (jp < blk))).astype(F32)
                bias = jnp.where((cnt < float(n_sel)) & (blk < qi), 0.0, NEG_INF)
            s_list = []
            for j in range(qi):
                s = lax.dot_general(k_ref[j], qh, _NT, preferred_element_type=F32)
                s_list.append(s if bias is None else s + bias[j:j + 1, :])
            s_list.append(jnp.where(causal, lax.dot_general(k_ref[qi], qh, _NT, preferred_element_type=F32),
                                    NEG_INF))
            live[qi, h] = s_list

    def numerators(qi):
        for h in range(heads):
            s_list = live[qi, h]
            m = jnp.max(functools.reduce(jnp.maximum, s_list), axis=0, keepdims=True)
            live[qi, h] = jnp.concatenate([jnp.exp2(s - m).astype(BF16) for s in s_list], axis=0)

    def outputs(qi):
        outs = []
        for h in range(heads):
            acc = jnp.dot(vt_ref[h * MB_VROWS:(h + 1) * MB_VROWS, 0:(qi + 1) * bk], live.pop((qi, h)),
                          preferred_element_type=F32)
            outs.append(acc[0:hd, :] * (1.0 / acc[hd:hd + 1, :]))
        o_ref[qi] = jnp.concatenate(outs, axis=0).T.astype(BF16)

    q_order = list(range(nb - 1, -1, -1))
    for step in range(nb + 2):
        if step < nb:
            scores(q_order[step])
        if 0 <= step - 1 < nb:
            numerators(q_order[step - 1])
        if 0 <= step - 2 < nb:
            outputs(q_order[step - 2])


def _moba(qb, kb, vb, batch, seq):
    nb = seq // MB_BLOCK
    assert nb % 2 == 0
    r5 = lambda a: a.reshape(MB_GROUPS, batch, nb, MB_BLOCK, V7X_LANES)
    spec = pl.BlockSpec((None, None, nb, MB_BLOCK, V7X_LANES), lambda b, p: (p, b, 0, 0, 0))
    est = 2 * 4 * seq * V7X_LANES * 2 + 2 * MB_VROWS * seq * 2 + 8 * seq * MB_BLOCK * 4
    out = pl.pallas_call(
        _moba_kernel,
        grid=(batch, MB_GROUPS),
        in_specs=[spec, spec, spec],
        out_specs=spec,
        out_shape=jax.ShapeDtypeStruct((MB_GROUPS, batch, nb, MB_BLOCK, V7X_LANES), BF16),
        scratch_shapes=[pltpu.VMEM((2 * MB_VROWS, seq), BF16),
                        pltpu.VMEM((nb, V7X_LANES), F32)],
        compiler_params=pltpu.CompilerParams(dimension_semantics=("arbitrary",) * 2,
                                             vmem_limit_bytes=_vmem_limit(est)),
        name="moba",
    )(r5(qb), r5(kb), r5(vb))
    return out.reshape(MB_GROUPS, batch * seq, V7X_LANES)


def _hgrn_kernel(qa_ref, logf_ref, kk_ref, v_ref, gate_ref, og_ref, o_ref):
    seq = qa_ref.shape[0]
    ch = HG_CHUNK
    tile = HG_GROUP * ch
    causal = (lax.broadcasted_iota(jnp.int32, (ch, ch), 0) >= lax.broadcasted_iota(jnp.int32, (ch, ch), 1))
    tri = causal.astype(BF16)
    rows = lax.broadcasted_iota(jnp.int32, (ch, 1), 0)
    og = og_ref[...]

    def step(ti, st_t):
        t0 = pl.multiple_of(ti * tile, tile)
        lf_t = logf_ref[pl.ds(t0, tile), :]
        hi_t = lf_t.astype(BF16)
        lo_t = (lf_t - hi_t.astype(F32)).astype(BF16)
        qa_t = qa_ref[pl.ds(t0, tile), :].astype(F32)
        kk_t = kk_ref[pl.ds(t0, tile), :].astype(F32)
        v_t = v_ref[pl.ds(t0, tile), :]
        gate_t = gate_ref[pl.ds(t0, tile), :].astype(F32)
        chunks = [slice(g * ch, (g + 1) * ch) for g in range(HG_GROUP)]
        cums = [jnp.dot(tri, hi_t[cs], preferred_element_type=F32)
                + jnp.dot(tri, lo_t[cs], preferred_element_type=F32) for cs in chunks]
        ops = []
        for cs, cum in zip(chunks, cums):
            qa, kk = qa_t[cs], kk_t[cs]
            last = cum[ch - 1:ch, :]
            qc = (qa * jnp.exp2(cum)).astype(BF16)
            kbar = (kk * jnp.exp2(last - cum)).astype(BF16)
            qts, kts = [], []
            for a in range(ch // HG_SUB):
                lo_r, hi_r = a * HG_SUB, (a + 1) * HG_SUB
                rho = cum[lo_r + HG_SUB // 2 - 1:lo_r + HG_SUB // 2, :]
                qts.append((qa[lo_r:hi_r] * jnp.exp2(cum[lo_r:hi_r] - rho)).astype(BF16))
                kts.append((kk * jnp.exp2(jnp.where(rows < hi_r, rho - cum, NEG_INF))).astype(BF16))
            ops.append((qc, kbar, qts, kts, jnp.exp2(last)))
        attns = []
        for qc, kbar, qts, kts, decay in ops:
            a_rows = [lax.dot_general(qt, kt, _NT, preferred_element_type=F32) for qt, kt in zip(qts, kts)]
            attns.append(jnp.where(causal, jnp.concatenate(a_rows, axis=0), 0.0).astype(BF16))
        o_intra = [jnp.dot(attn, v_t[cs], preferred_element_type=F32) for attn, cs in zip(attns, chunks)]
        incs = [lax.dot_general(v_t[cs], op[1], _TN, preferred_element_type=F32) for op, cs in zip(ops, chunks)]
        for g, cs in enumerate(chunks):
            qc, decay = ops[g][0], ops[g][4]
            o = o_intra[g] + lax.dot_general(qc, st_t.astype(BF16), _NT, preferred_element_type=F32)
            st_t = st_t * decay + incs[g]
            o_ref[pl.ds(t0 + g * ch, ch), :] = (_rms_norm(o, og) * gate_t[cs]).astype(BF16)
        return st_t

    lax.fori_loop(0, seq // tile, step, jnp.zeros((HG_DIM, HG_DIM), F32))


def _hgrn(qa, logf, kk, vh, gh, og, batch, seq):
    assert seq % (HG_GROUP * HG_CHUNK) == 0
    r4 = lambda a: a.reshape(HG_HEADS, batch, seq, HG_DIM)
    spec = pl.BlockSpec((None, None, seq, HG_DIM), lambda b, h: (h, b, 0, 0))
    est = 2 * seq * HG_DIM * (2 * 5 + 4)
    out = pl.pallas_call(
        _hgrn_kernel,
        grid=(batch, HG_HEADS),
        in_specs=[spec, spec, spec, spec, spec, pl.BlockSpec((1, HG_DIM), lambda b, h: (0, h))],
        out_specs=spec,
        out_shape=jax.ShapeDtypeStruct((HG_HEADS, batch, seq, HG_DIM), BF16),
        compiler_params=pltpu.CompilerParams(dimension_semantics=("arbitrary",) * 2,
                                             vmem_limit_bytes=_vmem_limit(est)),
        name="hgrn",
    )(r4(qa), r4(logf), r4(kk), r4(vh), r4(gh), og.reshape(1, HG_WIDTH))
    return out.reshape(HG_HEADS, batch * seq, HG_DIM)


def _merge_kernel(x_ref, oa_ref, ob_ref, g1_ref, wg_ref, wa_ref, wb_ref, wo_ref, g2_ref, x1_ref, h2_ref):
    x = x_ref[...]
    d = x.shape[1]
    ma = jnp.dot(_load_groups(oa_ref), wa_ref[...], preferred_element_type=F32)
    mb = jnp.dot(_load_groups(ob_ref), wb_ref[...], preferred_element_type=F32)
    h1 = _rms_norm(x, g1_ref[...]).astype(BF16)
    ga = jnp.dot(h1, wg_ref[:, 0:d], preferred_element_type=F32)
    gb = jnp.dot(h1, wg_ref[:, d:2 * d], preferred_element_type=F32)
    mix = _sigmoid(ga) * ma + _sigmoid(gb) * mb
    x1 = x + jnp.dot(mix.astype(BF16), wo_ref[...], preferred_element_type=F32)
    x1_ref[...] = x1
    h2_ref[...] = _rms_norm(x1, g2_ref[...]).astype(BF16)


def _merge(x2, oa, ob, g1, w_gate, w_a, w_b, w_out, g2):
    t, d = x2.shape
    tm = ROW_TILE
    row = lambda n: pl.BlockSpec((tm, n), lambda i: (i, 0))
    grouped = lambda a: pl.BlockSpec((a.shape[0], tm, V7X_LANES), lambda i: (0, i, 0))
    full = _resident
    est = (2 * tm * d * (4 + 4 + 2) + 4 * tm * HG_WIDTH * 2
           + 2 * (2 * d * d + 2 * HG_WIDTH * d + d * d) + 4 * tm * d * 4)
    return pl.pallas_call(
        _merge_kernel,
        grid=(t // tm,),
        in_specs=[row(d), grouped(oa), grouped(ob), full(g1), full(w_gate),
                  full(w_a), full(w_b), full(w_out), full(g2)],
        out_specs=[row(d), row(d)],
        out_shape=[jax.ShapeDtypeStruct((t, d), F32), jax.ShapeDtypeStruct((t, d), BF16)],
        compiler_params=pltpu.CompilerParams(dimension_semantics=("arbitrary",),
                                             vmem_limit_bytes=_vmem_limit(est)),
        name="merge",
    )(x2, oa, ob, g1, w_gate, w_a, w_b, w_out, g2)


def _ffn_kernel(h2_ref, x1_ref, wu_ref, cw_ref, cb_ref, wd_ref, o_ref, ubuf_ref, tail_ref, g_ref):
    tm = h2_ref.shape[0]
    pad = V7X_SUBLANES

    @pl.when(pl.program_id(1) == 0)
    def _():
        tail_ref[...] = jnp.zeros_like(tail_ref)

    h2 = h2_ref[...]
    inv_sqrt2 = float(1.0 / np.sqrt(2.0))
    for c in range(D_FF // FF_TILE):
        cs = slice(c * FF_TILE, (c + 1) * FF_TILE)
        u = jnp.dot(h2, wu_ref[:, cs], preferred_element_type=F32)
        v = jnp.dot(h2, wu_ref[:, D_FF + c * FF_TILE:D_FF + (c + 1) * FF_TILE], preferred_element_type=F32)
        ubuf_ref[0:pad, :] = tail_ref[c]
        ubuf_ref[pad:pad + tm, :] = u
        tail_ref[c] = u[tm - pad:tm, :]
        conv = (cb_ref[:, cs]
                + ubuf_ref[pad - 2:pad - 2 + tm, :] * cw_ref[0:1, cs]
                + ubuf_ref[pad - 1:pad - 1 + tm, :] * cw_ref[1:2, cs]
                + u * cw_ref[2:3, cs])
        gelu = 0.5 * conv * (1.0 + lax.erf(conv * inv_sqrt2))
        g_ref[:, cs] = (gelu * v).astype(BF16)
    o_ref[...] = x1_ref[...] + jnp.dot(g_ref[...], wd_ref[...], preferred_element_type=F32)


def _ffn(h2, x1, w_up, conv_w, conv_b, w_down, batch, seq):
    t, d = x1.shape
    tm = ROW_TILE
    n_s = seq // tm
    row = lambda n: pl.BlockSpec((tm, n), lambda b, s: (b * n_s + s, 0))
    full = _resident
    est = (2 * tm * d * (2 + 4 + 4) + 2 * (d * 2 * D_FF + D_FF * d)
           + (tm + 8) * FF_TILE * 4 + tm * D_FF * 2 + 8 * D_FF * 4 + 4 * tm * FF_TILE * 4)
    return pl.pallas_call(
        _ffn_kernel,
        grid=(batch, n_s),
        in_specs=[row(d), row(d), full(w_up), full(conv_w), full(conv_b), full(w_down)],
        out_specs=row(d),
        out_shape=jax.ShapeDtypeStruct((t, d), F32),
        scratch_shapes=[pltpu.VMEM((tm + V7X_SUBLANES, FF_TILE), F32),
                        pltpu.VMEM((D_FF // FF_TILE, V7X_SUBLANES, FF_TILE), F32),
                        pltpu.VMEM((tm, D_FF), BF16)],
        compiler_params=pltpu.CompilerParams(dimension_semantics=("arbitrary",) * 2,
                                             vmem_limit_bytes=_vmem_limit(est)),
        name="ffn",
    )(h2, x1, w_up, conv_w, conv_b, w_down)


def kernel(x, norm1_g, w_in, hg_lb_logits, hg_onorm_g, q_norm_g, k_norm_g, w_a, w_b, w_out,
           norm2_g, w_up, conv_w, conv_b, w_down):
    batch, seq, d = x.shape
    depth = w_in.shape[0]
    n_mix = 4 * HG_WIDTH + 3 * MB_WIDTH
    x2 = x.reshape(batch * seq, d)
    for l in range(depth):
        g1 = norm1_g[l][None, :]
        w_l = w_in[l].astype(BF16)
        qa, logf, kk, vh, gh, qb, kb, vb = _inproj(
            x2, g1, w_l[:, :n_mix], hg_lb_logits, q_norm_g[l], k_norm_g[l], seq, l)
        oa = _hgrn(qa, logf, kk, vh, gh, hg_onorm_g[l], batch, seq)
        ob = _moba(qb, kb, vb, batch, seq)
        x1, h2 = _merge(x2, oa, ob, g1, w_l[:, n_mix:], w_a[l].astype(BF16), w_b[l].astype(BF16),
                        w_out[l].astype(BF16), norm2_g[l][None, :])
        x2 = _ffn(h2, x1, w_up[l].astype(BF16), conv_w[l], conv_b[l][None, :],
                  w_down[l].astype(BF16), batch, seq)
    return x2.reshape(batch, seq, d)
```

```python
import functools

import numpy as np
import jax
import jax.numpy as jnp
from jax import lax
from jax.experimental import pallas as pl
from jax.experimental.pallas import tpu as pltpu

F32 = jnp.float32
BF16 = jnp.bfloat16

HG_HEADS = 4
HG_DIM = 128
HG_WIDTH = HG_HEADS * HG_DIM
HG_CHUNK = 64
MB_HEADS = 8
MB_HEAD_DIM = 64
MB_WIDTH = MB_HEADS * MB_HEAD_DIM
MB_BLOCK = 256
MB_TOPK = 3
ROPE_THETA = 10000.0
D_FF = 2816
CONV_WIDTH = 3
NORM_EPS = 1e-6

V7X_LANES = 128
V7X_SUBLANES = 8
V7X_BF16_ROWS = 16
V7X_VMEM_BYTES = 64 * 1024 * 1024

HG_SUB = 16
HG_GROUP = 32
ROW_TILE = 1024
FF_TILE = 256
NEG_INF = float("-inf")

MB_GROUPS = MB_WIDTH // V7X_LANES
MB_VROWS = MB_HEAD_DIM + V7X_BF16_ROWS

_NT = (((1,), (1,)), ((), ()))
_TN = (((0,), (0,)), ((), ()))


def _vmem_limit(nbytes):
    return int(min(nbytes * 3 // 2 + (4 << 20), V7X_VMEM_BYTES - (4 << 20)))


def _sigmoid(x):
    return 0.5 * jnp.tanh(0.5 * x) + 0.5


def _rms_norm(x, g):
    ms = jnp.mean(x * x, axis=-1, keepdims=True)
    return x * lax.rsqrt(ms + NORM_EPS) * g


def _resident(a):
    return pl.BlockSpec(a.shape, lambda *_: (0,) * a.ndim, pipeline_mode=pl.Buffered(1))


def _store_groups(ref, val):
    for gi in range(ref.shape[0]):
        ref[gi] = val[:, gi * V7X_LANES:(gi + 1) * V7X_LANES].astype(ref.dtype)


def _load_groups(ref):
    return jnp.concatenate([ref[gi] for gi in range(ref.shape[0])], axis=1)


def _inproj_kernel(x_ref, g1_ref, w_ref, lbl_ref, qg_ref, kg_ref, rc_ref, rs_ref, ones_ref,
                   qa_ref, logf_ref, kk_ref, vh_ref, gh_ref, qb_ref, kb_ref, vb_ref, *, layer):
    h = _rms_norm(x_ref[...], g1_ref[...]).astype(BF16)

    def proj(c0, width):
        return jnp.dot(h, w_ref[:, c0:c0 + width], preferred_element_type=F32)

    w = HG_WIDTH
    p = proj(0, w)
    _store_groups(qa_ref, p * _sigmoid(p))
    a = lbl_ref[...]
    e = jnp.exp(a - jnp.max(a, axis=0, keepdims=True))
    lb = jnp.sum(e[0:layer + 1, :], axis=0, keepdims=True) / jnp.sum(e, axis=0, keepdims=True)
    f = lb + (1.0 - lb) * _sigmoid(proj(w, w))
    _store_groups(logf_ref, jnp.log2(f))
    _store_groups(kk_ref, 1.0 - f)
    _store_groups(vh_ref, proj(2 * w, w))
    p = proj(3 * w, w)
    _store_groups(gh_ref, p * _sigmoid(p))

    lane = lax.broadcasted_iota(jnp.int32, (1, V7X_LANES), 1)
    first_half = (lane % MB_HEAD_DIM) < (MB_HEAD_DIM // 2)
    rc = rc_ref[...]
    rs = rs_ref[...]
    ones_bd = ones_ref[...]

    def norm_rope(p, g_ref, out_ref, scale):
        for c in range(MB_GROUPS):
            sl = slice(c * V7X_LANES, (c + 1) * V7X_LANES)
            pc = p[:, sl]
            ss = jnp.dot((pc * pc).astype(BF16), ones_bd, preferred_element_type=F32)
            y = pc * lax.rsqrt(ss * (1.0 / MB_HEAD_DIM) + NORM_EPS) * g_ref[:, sl]
            partner = jnp.where(first_half,
                                pltpu.roll(y, V7X_LANES - MB_HEAD_DIM // 2, 1),
                                pltpu.roll(y, MB_HEAD_DIM // 2, 1))
            out_ref[c] = ((y * rc + partner * rs) * scale).astype(BF16)

    base = 4 * w
    norm_rope(proj(base, MB_WIDTH), qg_ref, qb_ref, float(np.log2(np.e) / np.sqrt(MB_HEAD_DIM)))
    norm_rope(proj(base + MB_WIDTH, MB_WIDTH), kg_ref, kb_ref, 1.0)
    _store_groups(vb_ref, proj(base + 2 * MB_WIDTH, MB_WIDTH))


def _rope_tables(seq):
    half = MB_HEAD_DIM // 2
    inv = 1.0 / (ROPE_THETA ** (jnp.arange(half, dtype=F32) * 2.0 / MB_HEAD_DIM))
    ang = jnp.arange(seq).astype(F32)[:, None] * inv[None, :]
    cos = jnp.cos(ang)
    sin = jnp.sin(ang)
    rc = jnp.concatenate([cos, cos, cos, cos], axis=-1)
    rs = jnp.concatenate([-sin, sin, -sin, sin], axis=-1)
    return rc, rs


def _inproj(x2, g1, w_mix, lb_logits, qg, kg, seq, layer):
    t, d = x2.shape
    d_in = w_mix.shape[1]
    tm = ROW_TILE
    rc, rs = _rope_tables(seq)
    blk = np.arange(V7X_LANES) // MB_HEAD_DIM
    ones_bd = jnp.asarray(blk[:, None] == blk[None, :], dtype=BF16)
    qg_t = jnp.tile(qg.astype(F32), MB_HEADS)[None, :]
    kg_t = jnp.tile(kg.astype(F32), MB_HEADS)[None, :]
    n_pos = seq // tm

    full = _resident
    grouped = lambda n, dt: jax.ShapeDtypeStruct((n // V7X_LANES, t, V7X_LANES), dt)
    outs = [grouped(HG_WIDTH, BF16),
            grouped(HG_WIDTH, F32),
            grouped(HG_WIDTH, BF16),
            grouped(HG_WIDTH, BF16),
            grouped(HG_WIDTH, BF16),
            grouped(MB_WIDTH, BF16),
            grouped(MB_WIDTH, BF16),
            grouped(MB_WIDTH, BF16)]
    est = (2 * tm * d * 4 + d * d_in * 2 + 2 * tm * (d_in * 2 + HG_WIDTH * 4)
           + 4 * tm * V7X_LANES * 4 + 4 * tm * HG_WIDTH * 4)
    return pl.pallas_call(
        functools.partial(_inproj_kernel, layer=layer),
        grid=(t // tm,),
        in_specs=[pl.BlockSpec((tm, d), lambda i: (i, 0)),
                  full(g1), full(w_mix), full(lb_logits), full(qg_t), full(kg_t),
                  pl.BlockSpec((tm, V7X_LANES), lambda i: (i % n_pos, 0)),
                  pl.BlockSpec((tm, V7X_LANES), lambda i: (i % n_pos, 0)),
                  full(ones_bd)],
        out_specs=[pl.BlockSpec((o.shape[0], tm, V7X_LANES), lambda i: (0, i, 0)) for o in outs],
        out_shape=outs,
        compiler_params=pltpu.CompilerParams(dimension_semantics=("arbitrary",),
                                             vmem_limit_bytes=_vmem_limit(est)),
        name="inproj",
    )(x2, g1, w_mix, lb_logits, qg_t, kg_t, rc, rs, ones_bd)


def _moba_kernel(q_ref, k_ref, v_ref, o_ref, vt_ref, km_ref):
    nb, bk, _ = k_ref.shape
    bq = q_ref.shape[1]
    hd = MB_HEAD_DIM
    heads = V7X_LANES // hd
    n_sel = max(1, min(MB_TOPK, nb - 1))

    ones = jnp.ones((V7X_BF16_ROWS, bk), BF16)
    for j in range(nb):
        v_t = v_ref[j].astype(F32).T.astype(BF16)
        for h in range(heads):
            vt_ref[h * MB_VROWS:h * MB_VROWS + hd, j * bk:(j + 1) * bk] = v_t[h * hd:(h + 1) * hd, :]
            vt_ref[h * MB_VROWS + hd:(h + 1) * MB_VROWS, j * bk:(j + 1) * bk] = ones
        km_ref[j:j + 1, :] = jnp.mean(k_ref[j].astype(F32), axis=0, keepdims=True)

    lane = lax.broadcasted_iota(jnp.int32, (1, V7X_LANES), 1)
    causal = (lax.broadcasted_iota(jnp.int32, (bk, bq), 0) <= lax.broadcasted_iota(jnp.int32, (bk, bq), 1))
    blk = lax.broadcasted_iota(jnp.int32, (nb, bq), 0)
    live = {}

    def scores(qi):
        q = q_ref[qi]
        for h in range(heads):
            qh = jnp.where((lane // hd) == h, q, jnp.zeros_like(q))
            bias = None
            if qi > n_sel:
                g = lax.dot_general(km_ref[...].astype(BF16), qh, _NT, preferred_element_type=F32)
                cnt = jnp.zeros((nb, bq), F32)
                for jp in range(qi):
                    gj = g[jp:jp + 1, :]
                    cnt = cnt + ((gj > g) | ((gj == g) & (jp < blk))).astype(F32)
                bias = jnp.where((cnt < float(n_sel)) & (blk < qi), 0.0, NEG_INF).astype(BF16)
            s_list = []
            for j in range(qi):
                s = lax.dot_general(k_ref[j], qh, _NT, preferred_element_type=F32).astype(BF16)
                s_list.append(s if bias is None else s + bias[j:j + 1, :])
            s_own = lax.dot_general(k_ref[qi], qh, _NT, preferred_element_type=F32).astype(BF16)
            s_list.append(jnp.where(causal, s_own, jnp.full_like(s_own, NEG_INF)))
            live[qi, h] = s_list

    def numerators(qi):
        for h in range(heads):
            s_list = live[qi, h]
            m = jnp.max(functools.reduce(jnp.maximum, s_list), axis=0, keepdims=True)
            live[qi, h] = jnp.concatenate([jnp.exp2(s - m) for s in s_list], axis=0)

    def outputs(qi):
        outs = []
        for h in range(heads):
            acc = jnp.dot(vt_ref[h * MB_VROWS:(h + 1) * MB_VROWS, 0:(qi + 1) * bk], live.pop((qi, h)),
                          preferred_element_type=F32)
            outs.append(acc[0:hd, :] * (1.0 / acc[hd:hd + 1, :]))
        o_ref[qi] = jnp.concatenate(outs, axis=0).T.astype(BF16)

    q_order = list(range(nb - 1, -1, -1))
    for step in range(nb + 2):
        if step < nb:
            scores(q_order[step])
        if 0 <= step - 1 < nb:
            numerators(q_order[step - 1])
        if 0 <= step - 2 < nb:
            outputs(q_order[step - 2])


def _moba(qb, kb, vb, batch, seq):
    nb = seq // MB_BLOCK
    assert nb % 2 == 0
    r5 = lambda a: a.reshape(MB_GROUPS, batch, nb, MB_BLOCK, V7X_LANES)
    spec = pl.BlockSpec((None, None, nb, MB_BLOCK, V7X_LANES), lambda b, p: (p, b, 0, 0, 0))
    est = 2 * 4 * seq * V7X_LANES * 2 + 2 * MB_VROWS * seq * 2 + 8 * seq * MB_BLOCK * 4
    out = pl.pallas_call(
        _moba_kernel,
        grid=(batch, MB_GROUPS),
        in_specs=[spec, spec, spec],
        out_specs=spec,
        out_shape=jax.ShapeDtypeStruct((MB_GROUPS, batch, nb, MB_BLOCK, V7X_LANES), BF16),
        scratch_shapes=[pltpu.VMEM((2 * MB_VROWS, seq), BF16),
                        pltpu.VMEM((nb, V7X_LANES), F32)],
        compiler_params=pltpu.CompilerParams(dimension_semantics=("arbitrary",) * 2,
                                             vmem_limit_bytes=_vmem_limit(est)),
        name="moba",
    )(r5(qb), r5(kb), r5(vb))
    return out.reshape(MB_GROUPS, batch * seq, V7X_LANES)


def _hgrn_kernel(qa_ref, logf_ref, kk_ref, v_ref, gate_ref, og_ref, o_ref):
    seq = qa_ref.shape[0]
    ch = HG_CHUNK
    tile = HG_GROUP * ch
    causal = (lax.broadcasted_iota(jnp.int32, (ch, ch), 0) >= lax.broadcasted_iota(jnp.int32, (ch, ch), 1))
    tri = causal.astype(BF16)
    og = og_ref[...]

    def step(ti, st_t):
        t0 = pl.multiple_of(ti * tile, tile)
        lf_t = logf_ref[pl.ds(t0, tile), :]
        hi_t = lf_t.astype(BF16)
        lo_t = (lf_t - hi_t.astype(F32)).astype(BF16)
        qa_t = qa_ref[pl.ds(t0, tile), :].astype(F32)
        kk_t = kk_ref[pl.ds(t0, tile), :].astype(F32)
        v_t = v_ref[pl.ds(t0, tile), :]
        gate_t = gate_ref[pl.ds(t0, tile), :].astype(F32)
        chunks = [slice(g * ch, (g + 1) * ch) for g in range(HG_GROUP)]
        cums = [jnp.dot(tri, hi_t[cs], preferred_element_type=F32)
                + jnp.dot(tri, lo_t[cs], preferred_element_type=F32) for cs in chunks]
        ops = []
        for cs, cum in zip(chunks, cums):
            qa, kk = qa_t[cs], kk_t[cs]
            last = cum[ch - 1:ch, :]
            qc = (qa * jnp.exp2(cum)).astype(BF16)
            kbar = (kk * jnp.exp2(last - cum)).astype(BF16)
            qts, kts = [], []
            for a in range(ch // HG_SUB):
                lo_r, hi_r = a * HG_SUB, (a + 1) * HG_SUB
                rho = cum[lo_r + HG_SUB // 2 - 1:lo_r + HG_SUB // 2, :]
                qts.append((qa[lo_r:hi_r] * jnp.exp2(cum[lo_r:hi_r] - rho)).astype(BF16))
                kt = (kk[0:hi_r] * jnp.exp2(rho - cum[0:hi_r])).astype(BF16)
                if hi_r < ch:
                    kt = jnp.concatenate([kt, jnp.zeros((ch - hi_r, HG_DIM), BF16)], axis=0)
                kts.append(kt)
            ops.append((qc, kbar, qts, kts, jnp.exp2(last)))
        attns = []
        for qc, kbar, qts, kts, decay in ops:
            a_rows = [lax.dot_general(qt, kt, _NT, preferred_element_type=F32) for qt, kt in zip(qts, kts)]
            attns.append(jnp.where(causal, jnp.concatenate(a_rows, axis=0), 0.0).astype(BF16))
        o_intra = [jnp.dot(attn, v_t[cs], preferred_element_type=F32) for attn, cs in zip(attns, chunks)]
        incs = [lax.dot_general(v_t[cs], op[1], _TN, preferred_element_type=F32) for op, cs in zip(ops, chunks)]
        for g, cs in enumerate(chunks):
            qc, decay = ops[g][0], ops[g][4]
            o = o_intra[g] + lax.dot_general(qc, st_t.astype(BF16), _NT, preferred_element_type=F32)
            st_t = st_t * decay + incs[g]
            o_ref[pl.ds(t0 + g * ch, ch), :] = (_rms_norm(o, og) * gate_t[cs]).astype(BF16)
        return st_t

    lax.fori_loop(0, seq // tile, step, jnp.zeros((HG_DIM, HG_DIM), F32))


def _hgrn(qa, logf, kk, vh, gh, og, batch, seq):
    assert seq % (HG_GROUP * HG_CHUNK) == 0
    r4 = lambda a: a.reshape(HG_HEADS, batch, seq, HG_DIM)
    spec = pl.BlockSpec((None, None, seq, HG_DIM), lambda b, h: (h, b, 0, 0))
    est = 2 * seq * HG_DIM * (2 * 5 + 4)
    out = pl.pallas_call(
        _hgrn_kernel,
        grid=(batch, HG_HEADS),
        in_specs=[spec, spec, spec, spec, spec, pl.BlockSpec((1, HG_DIM), lambda b, h: (0, h))],
        out_specs=spec,
        out_shape=jax.ShapeDtypeStruct((HG_HEADS, batch, seq, HG_DIM), BF16),
        compiler_params=pltpu.CompilerParams(dimension_semantics=("arbitrary",) * 2,
                                             vmem_limit_bytes=_vmem_limit(est)),
        name="hgrn",
    )(r4(qa), r4(logf), r4(kk), r4(vh), r4(gh), og.reshape(1, HG_WIDTH))
    return out.reshape(HG_HEADS, batch * seq, HG_DIM)


def _merge_kernel(x_ref, oa_ref, ob_ref, g1_ref, wg_ref, wa_ref, wb_ref, wo_ref, g2_ref, x1_ref, h2_ref):
    x = x_ref[...]
    d = x.shape[1]
    ma = jnp.dot(_load_groups(oa_ref), wa_ref[...], preferred_element_type=F32)
    mb = jnp.dot(_load_groups(ob_ref), wb_ref[...], preferred_element_type=F32)
    h1 = _rms_norm(x, g1_ref[...]).astype(BF16)
    ga = jnp.dot(h1, wg_ref[:, 0:d], preferred_element_type=F32)
    gb = jnp.dot(h1, wg_ref[:, d:2 * d], preferred_element_type=F32)
    mix = _sigmoid(ga) * ma + _sigmoid(gb) * mb
    x1 = x + jnp.dot(mix.astype(BF16), wo_ref[...], preferred_element_type=F32)
    x1_ref[...] = x1
    h2_ref[...] = _rms_norm(x1, g2_ref[...]).astype(BF16)


def _merge(x2, oa, ob, g1, w_gate, w_a, w_b, w_out, g2):
    t, d = x2.shape
    tm = ROW_TILE
    row = lambda n: pl.BlockSpec((tm, n), lambda i: (i, 0))
    grouped = lambda a: pl.BlockSpec((a.shape[0], tm, V7X_LANES), lambda i: (0, i, 0))
    full = _resident
    est = (2 * tm * d * (4 + 4 + 2) + 4 * tm * HG_WIDTH * 2
           + 2 * (2 * d * d + 2 * HG_WIDTH * d + d * d) + 4 * tm * d * 4)
    return pl.pallas_call(
        _merge_kernel,
        grid=(t // tm,),
        in_specs=[row(d), grouped(oa), grouped(ob), full(g1), full(w_gate),
                  full(w_a), full(w_b), full(w_out), full(g2)],
        out_specs=[row(d), row(d)],
        out_shape=[jax.ShapeDtypeStruct((t, d), F32), jax.ShapeDtypeStruct((t, d), BF16)],
        compiler_params=pltpu.CompilerParams(dimension_semantics=("arbitrary",),
                                             vmem_limit_bytes=_vmem_limit(est)),
        name="merge",
    )(x2, oa, ob, g1, w_gate, w_a, w_b, w_out, g2)


def _ffn_kernel(h2_ref, x1_ref, wu_ref, cw_ref, cb_ref, wd_ref, o_ref, ubuf_ref, tail_ref, g_ref):
    tm = h2_ref.shape[0]
    pad = V7X_SUBLANES

    @pl.when(pl.program_id(1) == 0)
    def _():
        tail_ref[...] = jnp.zeros_like(tail_ref)

    h2 = h2_ref[...]
    inv_sqrt2 = float(1.0 / np.sqrt(2.0))
    for c in range(D_FF // FF_TILE):
        cs = slice(c * FF_TILE, (c + 1) * FF_TILE)
        u = jnp.dot(h2, wu_ref[:, cs], preferred_element_type=F32)
        v = jnp.dot(h2, wu_ref[:, D_FF + c * FF_TILE:D_FF + (c + 1) * FF_TILE], preferred_element_type=F32)
        ubuf_ref[0:pad, :] = tail_ref[c]
        ubuf_ref[pad:pad + tm, :] = u
        tail_ref[c] = u[tm - pad:tm, :]
        conv = (cb_ref[:, cs]
                + ubuf_ref[pad - 2:pad - 2 + tm, :] * cw_ref[0:1, cs]
                + ubuf_ref[pad - 1:pad - 1 + tm, :] * cw_ref[1:2, cs]
                + u * cw_ref[2:3, cs])
        gelu = 0.5 * conv * (1.0 + lax.erf(conv * inv_sqrt2))
        g_ref[:, cs] = (gelu * v).astype(BF16)
    o_ref[...] = x1_ref[...] + jnp.dot(g_ref[...], wd_ref[...], preferred_element_type=F32)


def _ffn(h2, x1, w_up, conv_w, conv_b, w_down, batch, seq):
    t, d = x1.shape
    tm = ROW_TILE
    n_s = seq // tm
    row = lambda n: pl.BlockSpec((tm, n), lambda b, s: (b * n_s + s, 0))
    full = _resident
    est = (2 * tm * d * (2 + 4 + 4) + 2 * (d * 2 * D_FF + D_FF * d)
           + (tm + 8) * FF_TILE * 4 + tm * D_FF * 2 + 8 * D_FF * 4 + 4 * tm * FF_TILE * 4)
    return pl.pallas_call(
        _ffn_kernel,
        grid=(batch, n_s),
        in_specs=[row(d), row(d), full(w_up), full(conv_w), full(conv_b), full(w_down)],
        out_specs=row(d),
        out_shape=jax.ShapeDtypeStruct((t, d), F32),
        scratch_shapes=[pltpu.VMEM((tm + V7X_SUBLANES, FF_TILE), F32),
                        pltpu.VMEM((D_FF // FF_TILE, V7X_SUBLANES, FF_TILE), F32),
                        pltpu.VMEM((tm, D_FF), BF16)],
        compiler_params=pltpu.CompilerParams(dimension_semantics=("arbitrary",) * 2,
                                             vmem_limit_bytes=_vmem_limit(est)),
        name="ffn",
    )(h2, x1, w_up, conv_w, conv_b, w_down)


def kernel(x, norm1_g, w_in, hg_lb_logits, hg_onorm_g, q_norm_g, k_norm_g, w_a, w_b, w_out,
           norm2_g, w_up, conv_w, conv_b, w_down):
    batch, seq, d = x.shape
    depth = w_in.shape[0]
    n_mix = 4 * HG_WIDTH + 3 * MB_WIDTH
    x2 = x.reshape(batch * seq, d)
    for l in range(depth):
        g1 = norm1_g[l][None, :]
        w_l = w_in[l].astype(BF16)
        qa, logf, kk, vh, gh, qb, kb, vb = _inproj(
            x2, g1, w_l[:, :n_mix], hg_lb_logits, q_norm_g[l], k_norm_g[l], seq, l)
        oa = _hgrn(qa, logf, kk, vh, gh, hg_onorm_g[l], batch, seq)
        ob = _moba(qb, kb, vb, batch, seq)
        x1, h2 = _merge(x2, oa, ob, g1, w_l[:, n_mix:], w_a[l].astype(BF16), w_b[l].astype(BF16),
                        w_out[l].astype(BF16), norm2_g[l][None, :])
        x2 = _ffn(h2, x1, w_up[l].astype(BF16), conv_w[l], conv_b[l][None, :],
                  w_down[l].astype(BF16), batch, seq)
    return x2.reshape(batch, seq, d)
```

```python
import functools

import numpy as np
import jax
import jax.numpy as jnp
from jax import lax
from jax.experimental import pallas as pl
from jax.experimental.pallas import tpu as pltpu

F32 = jnp.float32
BF16 = jnp.bfloat16

HG_HEADS = 4
HG_DIM = 128
HG_WIDTH = HG_HEADS * HG_DIM
HG_CHUNK = 64
MB_HEADS = 8
MB_HEAD_DIM = 64
MB_WIDTH = MB_HEADS * MB_HEAD_DIM
MB_BLOCK = 256
MB_TOPK = 3
ROPE_THETA = 10000.0
D_FF = 2816
CONV_WIDTH = 3
NORM_EPS = 1e-6

V7X_LANES = 128
V7X_SUBLANES = 8
V7X_BF16_ROWS = 16
V7X_VMEM_BYTES = 64 * 1024 * 1024

HG_SUB = 16
ROW_TILE = 1024
FF_TILE = 256
NEG_INF = float("-inf")

MB_GROUPS = MB_WIDTH // V7X_LANES
MB_VROWS = MB_HEAD_DIM + V7X_BF16_ROWS

_NT = (((1,), (1,)), ((), ()))
_TN = (((0,), (0,)), ((), ()))


def _vmem_limit(nbytes):
    return int(min(nbytes * 3 // 2 + (4 << 20), V7X_VMEM_BYTES - (4 << 20)))


def _sigmoid(x):
    return 0.5 * jnp.tanh(0.5 * x) + 0.5


def _rms_norm(x, g):
    ms = jnp.mean(x * x, axis=-1, keepdims=True)
    return x * lax.rsqrt(ms + NORM_EPS) * g


def _resident(a):
    return pl.BlockSpec(a.shape, lambda *_: (0,) * a.ndim, pipeline_mode=pl.Buffered(1))


def _store_groups(ref, val):
    for gi in range(ref.shape[0]):
        ref[gi] = val[:, gi * V7X_LANES:(gi + 1) * V7X_LANES].astype(ref.dtype)


def _load_groups(ref):
    return jnp.concatenate([ref[gi] for gi in range(ref.shape[0])], axis=1)


def _inproj_kernel(x_ref, g1_ref, w_ref, lbl_ref, qg_ref, kg_ref, rc_ref, rs_ref, ones_ref,
                   qa_ref, logf_ref, kk_ref, vh_ref, gh_ref, qb_ref, kb_ref, vb_ref, *, layer):
    h = _rms_norm(x_ref[...], g1_ref[...]).astype(BF16)

    def proj(c0, width):
        return jnp.dot(h, w_ref[:, c0:c0 + width], preferred_element_type=F32)

    w = HG_WIDTH
    p = proj(0, w)
    _store_groups(qa_ref, p * _sigmoid(p))
    a = lbl_ref[...]
    e = jnp.exp(a - jnp.max(a, axis=0, keepdims=True))
    lb = jnp.sum(e[0:layer + 1, :], axis=0, keepdims=True) / jnp.sum(e, axis=0, keepdims=True)
    f = lb + (1.0 - lb) * _sigmoid(proj(w, w))
    _store_groups(logf_ref, jnp.log2(f))
    _store_groups(kk_ref, 1.0 - f)
    _store_groups(vh_ref, proj(2 * w, w))
    p = proj(3 * w, w)
    _store_groups(gh_ref, p * _sigmoid(p))

    lane = lax.broadcasted_iota(jnp.int32, (1, V7X_LANES), 1)
    first_half = (lane % MB_HEAD_DIM) < (MB_HEAD_DIM // 2)
    rc = rc_ref[...]
    rs = rs_ref[...]
    ones_bd = ones_ref[...]

    def norm_rope(p, g_ref, out_ref, scale):
        for c in range(MB_GROUPS):
            sl = slice(c * V7X_LANES, (c + 1) * V7X_LANES)
            pc = p[:, sl]
            ss = jnp.dot((pc * pc).astype(BF16), ones_bd, preferred_element_type=F32)
            y = pc * lax.rsqrt(ss * (1.0 / MB_HEAD_DIM) + NORM_EPS) * g_ref[:, sl]
            partner = jnp.where(first_half,
                                pltpu.roll(y, V7X_LANES - MB_HEAD_DIM // 2, 1),
                                pltpu.roll(y, MB_HEAD_DIM // 2, 1))
            out_ref[c] = ((y * rc + partner * rs) * scale).astype(BF16)

    base = 4 * w
    norm_rope(proj(base, MB_WIDTH), qg_ref, qb_ref, float(np.log2(np.e) / np.sqrt(MB_HEAD_DIM)))
    norm_rope(proj(base + MB_WIDTH, MB_WIDTH), kg_ref, kb_ref, 1.0)
    _store_groups(vb_ref, proj(base + 2 * MB_WIDTH, MB_WIDTH))


def _rope_tables(seq):
    half = MB_HEAD_DIM // 2
    inv = 1.0 / (ROPE_THETA ** (jnp.arange(half, dtype=F32) * 2.0 / MB_HEAD_DIM))
    ang = jnp.arange(seq).astype(F32)[:, None] * inv[None, :]
    cos = jnp.cos(ang)
    sin = jnp.sin(ang)
    rc = jnp.concatenate([cos, cos, cos, cos], axis=-1)
    rs = jnp.concatenate([-sin, sin, -sin, sin], axis=-1)
    return rc, rs


def _inproj(x2, g1, w_mix, lb_logits, qg, kg, seq, layer):
    t, d = x2.shape
    d_in = w_mix.shape[1]
    tm = ROW_TILE
    rc, rs = _rope_tables(seq)
    blk = np.arange(V7X_LANES) // MB_HEAD_DIM
    ones_bd = jnp.asarray(blk[:, None] == blk[None, :], dtype=BF16)
    qg_t = jnp.tile(qg.astype(F32), MB_HEADS)[None, :]
    kg_t = jnp.tile(kg.astype(F32), MB_HEADS)[None, :]
    n_pos = seq // tm

    full = _resident
    grouped = lambda n, dt: jax.ShapeDtypeStruct((n // V7X_LANES, t, V7X_LANES), dt)
    outs = [grouped(HG_WIDTH, BF16),
            grouped(HG_WIDTH, F32),
            grouped(HG_WIDTH, BF16),
            grouped(HG_WIDTH, BF16),
            grouped(HG_WIDTH, BF16),
            grouped(MB_WIDTH, BF16),
            grouped(MB_WIDTH, BF16),
            grouped(MB_WIDTH, BF16)]
    est = (2 * tm * d * 4 + d * d_in * 2 + 2 * tm * (d_in * 2 + HG_WIDTH * 4)
           + 4 * tm * V7X_LANES * 4 + 4 * tm * HG_WIDTH * 4)
    return pl.pallas_call(
        functools.partial(_inproj_kernel, layer=layer),
        grid=(t // tm,),
        in_specs=[pl.BlockSpec((tm, d), lambda i: (i, 0)),
                  full(g1), full(w_mix), full(lb_logits), full(qg_t), full(kg_t),
                  pl.BlockSpec((tm, V7X_LANES), lambda i: (i % n_pos, 0)),
                  pl.BlockSpec((tm, V7X_LANES), lambda i: (i % n_pos, 0)),
                  full(ones_bd)],
        out_specs=[pl.BlockSpec((o.shape[0], tm, V7X_LANES), lambda i: (0, i, 0)) for o in outs],
        out_shape=outs,
        compiler_params=pltpu.CompilerParams(dimension_semantics=("arbitrary",),
                                             vmem_limit_bytes=_vmem_limit(est)),
        name="inproj",
    )(x2, g1, w_mix, lb_logits, qg_t, kg_t, rc, rs, ones_bd)


def _moba_kernel(q_ref, k_ref, v_ref, o_ref, vt_ref, km_ref):
    groups, nb, bk, _ = k_ref.shape
    bq = q_ref.shape[2]
    hd = MB_HEAD_DIM
    heads = V7X_LANES // hd
    n_sel = max(1, min(MB_TOPK, nb - 1))

    ones = jnp.ones((V7X_BF16_ROWS, bk), BF16)
    for gp in range(groups):
        for j in range(nb):
            v_t = v_ref[gp, j].astype(F32).T.astype(BF16)
            for h in range(heads):
                vt_ref[gp, h * MB_VROWS:h * MB_VROWS + hd, j * bk:(j + 1) * bk] = v_t[h * hd:(h + 1) * hd, :]
                vt_ref[gp, h * MB_VROWS + hd:(h + 1) * MB_VROWS, j * bk:(j + 1) * bk] = ones
            km_ref[gp, j:j + 1, :] = jnp.mean(k_ref[gp, j].astype(F32), axis=0, keepdims=True)

    lane = lax.broadcasted_iota(jnp.int32, (1, V7X_LANES), 1)
    causal = (lax.broadcasted_iota(jnp.int32, (bk, bq), 0) <= lax.broadcasted_iota(jnp.int32, (bk, bq), 1))
    blk = lax.broadcasted_iota(jnp.int32, (nb, bq), 0)
    live = {}

    def scores(gp, qi):
        q = q_ref[gp, qi]
        for h in range(heads):
            qh = jnp.where((lane // hd) == h, q, jnp.zeros_like(q))
            bias = None
            if qi > n_sel:
                g = lax.dot_general(km_ref[gp].astype(BF16), qh, _NT, preferred_element_type=F32)
                cnt = jnp.zeros((nb, bq), F32)
                for jp in range(qi):
                    gj = g[jp:jp + 1, :]
                    cnt = cnt + ((gj > g) | ((gj == g) & (jp < blk))).astype(F32)
                bias = jnp.where((cnt < float(n_sel)) & (blk < qi), 0.0, NEG_INF).astype(BF16)
            s_list = []
            for j in range(qi):
                s = lax.dot_general(k_ref[gp, j], qh, _NT, preferred_element_type=F32).astype(BF16)
                s_list.append(s if bias is None else s + bias[j:j + 1, :])
            s_own = lax.dot_general(k_ref[gp, qi], qh, _NT, preferred_element_type=F32).astype(BF16)
            s_list.append(jnp.where(causal, s_own, jnp.full_like(s_own, NEG_INF)))
            live[gp, qi, h] = s_list

    def numerators(gp, qi):
        for h in range(heads):
            s_list = live[gp, qi, h]
            m = jnp.max(functools.reduce(jnp.maximum, s_list), axis=0, keepdims=True)
            live[gp, qi, h] = jnp.concatenate([jnp.exp2(s - m) for s in s_list], axis=0)

    def outputs(gp, qi):
        outs = []
        for h in range(heads):
            acc = jnp.dot(vt_ref[gp, h * MB_VROWS:(h + 1) * MB_VROWS, 0:(qi + 1) * bk], live.pop((gp, qi, h)),
                          preferred_element_type=F32)
            outs.append(acc[0:hd, :] * (1.0 / acc[hd:hd + 1, :]))
        o_ref[gp, qi] = jnp.concatenate(outs, axis=0).T.astype(BF16)

    q_order = list(range(nb - 1, -1, -1))
    for step in range(nb + 2):
        for gp in range(groups):
            if step < nb:
                scores(gp, q_order[step])
            if 0 <= step - 1 < nb:
                numerators(gp, q_order[step - 1])
            if 0 <= step - 2 < nb:
                outputs(gp, q_order[step - 2])


def _moba(qb, kb, vb, batch, seq):
    nb = seq // MB_BLOCK
    r5 = lambda a: a.reshape(MB_GROUPS, batch, nb, MB_BLOCK, V7X_LANES)
    spec = pl.BlockSpec((MB_GROUPS, None, nb, MB_BLOCK, V7X_LANES), lambda b: (0, b, 0, 0, 0))
    est = MB_GROUPS * (2 * 4 * seq * V7X_LANES * 2 + 2 * MB_VROWS * seq * 2 + 4 * seq * MB_BLOCK * 4)
    out = pl.pallas_call(
        _moba_kernel,
        grid=(batch,),
        in_specs=[spec, spec, spec],
        out_specs=spec,
        out_shape=jax.ShapeDtypeStruct((MB_GROUPS, batch, nb, MB_BLOCK, V7X_LANES), BF16),
        scratch_shapes=[pltpu.VMEM((MB_GROUPS, 2 * MB_VROWS, seq), BF16),
                        pltpu.VMEM((MB_GROUPS, nb, V7X_LANES), F32)],
        compiler_params=pltpu.CompilerParams(dimension_semantics=("arbitrary",),
                                             vmem_limit_bytes=_vmem_limit(est)),
        name="moba",
    )(r5(qb), r5(kb), r5(vb))
    return out.reshape(MB_GROUPS, batch * seq, V7X_LANES)


def _hgrn_kernel(qa_ref, logf_ref, kk_ref, v_ref, gate_ref, og_ref, o_ref):
    heads, seq, _ = qa_ref.shape
    ch = HG_CHUNK
    causal = (lax.broadcasted_iota(jnp.int32, (ch, ch), 0) >= lax.broadcasted_iota(jnp.int32, (ch, ch), 1))
    tri = causal.astype(BF16)
    chunks = [slice(g * ch, (g + 1) * ch) for g in range(seq // ch)]
    units = [(hh, cs) for hh in range(heads) for cs in chunks]

    lf = [logf_ref[hh] for hh in range(heads)]
    hi = [x.astype(BF16) for x in lf]
    lo = [(x - h.astype(F32)).astype(BF16) for x, h in zip(lf, hi)]
    qa_h = [qa_ref[hh].astype(F32) for hh in range(heads)]
    kk_h = [kk_ref[hh].astype(F32) for hh in range(heads)]
    v_h = [v_ref[hh] for hh in range(heads)]
    cums = [jnp.dot(tri, hi[hh][cs], preferred_element_type=F32)
            + jnp.dot(tri, lo[hh][cs], preferred_element_type=F32) for hh, cs in units]
    ops = []
    for (hh, cs), cum in zip(units, cums):
        qa, kk = qa_h[hh][cs], kk_h[hh][cs]
        last = cum[ch - 1:ch, :]
        qc = (qa * jnp.exp2(cum)).astype(BF16)
        kbar = (kk * jnp.exp2(last - cum)).astype(BF16)
        qts, kts = [], []
        for a in range(ch // HG_SUB):
            lo_r, hi_r = a * HG_SUB, (a + 1) * HG_SUB
            rho = cum[lo_r + HG_SUB // 2 - 1:lo_r + HG_SUB // 2, :]
            qts.append((qa[lo_r:hi_r] * jnp.exp2(cum[lo_r:hi_r] - rho)).astype(BF16))
            kt = (kk[0:hi_r] * jnp.exp2(rho - cum[0:hi_r])).astype(BF16)
            if hi_r < ch:
                kt = jnp.concatenate([kt, jnp.zeros((ch - hi_r, HG_DIM), BF16)], axis=0)
            kts.append(kt)
        ops.append((qc, kbar, qts, kts, jnp.exp2(last)))
    attns = []
    for qc, kbar, qts, kts, decay in ops:
        a_rows = [lax.dot_general(qt, kt, _NT, preferred_element_type=F32) for qt, kt in zip(qts, kts)]
        attns.append(jnp.where(causal, jnp.concatenate(a_rows, axis=0), 0.0).astype(BF16))
    o_intra = [jnp.dot(attn, v_h[hh][cs], preferred_element_type=F32) for attn, (hh, cs) in zip(attns, units)]
    incs = [lax.dot_general(v_h[hh][cs], op[1], _TN, preferred_element_type=F32) for op, (hh, cs) in zip(ops, units)]
    state = [jnp.zeros((HG_DIM, HG_DIM), F32) for _ in range(heads)]
    for g, cs in enumerate(chunks):
        for hh in range(heads):
            i = hh * len(chunks) + g
            qc, decay = ops[i][0], ops[i][4]
            o = o_intra[i] + lax.dot_general(qc, state[hh].astype(BF16), _NT, preferred_element_type=F32)
            state[hh] = state[hh] * decay + incs[i]
            o_ref[hh, cs, :] = (_rms_norm(o, og_ref[hh]) * gate_ref[hh, cs, :].astype(F32)).astype(BF16)


def _hgrn(qa, logf, kk, vh, gh, og, batch, seq):
    assert seq % HG_CHUNK == 0
    r4 = lambda a: a.reshape(HG_HEADS, batch, seq, HG_DIM)
    spec = pl.BlockSpec((HG_HEADS, None, seq, HG_DIM), lambda b: (0, b, 0, 0))
    og3 = og.reshape(HG_HEADS, 1, HG_DIM)
    est = 2 * HG_HEADS * seq * HG_DIM * (2 * 5 + 4) + 8 * HG_HEADS * seq * HG_DIM * 4
    out = pl.pallas_call(
        _hgrn_kernel,
        grid=(batch,),
        in_specs=[spec, spec, spec, spec, spec, _resident(og3)],
        out_specs=spec,
        out_shape=jax.ShapeDtypeStruct((HG_HEADS, batch, seq, HG_DIM), BF16),
        compiler_params=pltpu.CompilerParams(dimension_semantics=("arbitrary",),
                                             vmem_limit_bytes=_vmem_limit(est)),
        name="hgrn",
    )(r4(qa), r4(logf), r4(kk), r4(vh), r4(gh), og3)
    return out.reshape(HG_HEADS, batch * seq, HG_DIM)


def _merge_kernel(x_ref, oa_ref, ob_ref, g1_ref, wg_ref, wa_ref, wb_ref, wo_ref, g2_ref, x1_ref, h2_ref):
    x = x_ref[...]
    d = x.shape[1]
    ma = jnp.dot(_load_groups(oa_ref), wa_ref[...], preferred_element_type=F32)
    mb = jnp.dot(_load_groups(ob_ref), wb_ref[...], preferred_element_type=F32)
    h1 = _rms_norm(x, g1_ref[...]).astype(BF16)
    ga = jnp.dot(h1, wg_ref[:, 0:d], preferred_element_type=F32)
    gb = jnp.dot(h1, wg_ref[:, d:2 * d], preferred_element_type=F32)
    mix = _sigmoid(ga) * ma + _sigmoid(gb) * mb
    x1 = x + jnp.dot(mix.astype(BF16), wo_ref[...], preferred_element_type=F32)
    x1_ref[...] = x1
    h2_ref[...] = _rms_norm(x1, g2_ref[...]).astype(BF16)


def _merge(x2, oa, ob, g1, w_gate, w_a, w_b, w_out, g2):
    t, d = x2.shape
    tm = ROW_TILE
    row = lambda n: pl.BlockSpec((tm, n), lambda i: (i, 0))
    grouped = lambda a: pl.BlockSpec((a.shape[0], tm, V7X_LANES), lambda i: (0, i, 0))
    full = _resident
    est = (2 * tm * d * (4 + 4 + 2) + 4 * tm * HG_WIDTH * 2
           + 2 * (2 * d * d + 2 * HG_WIDTH * d + d * d) + 4 * tm * d * 4)
    return pl.pallas_call(
        _merge_kernel,
        grid=(t // tm,),
        in_specs=[row(d), grouped(oa), grouped(ob), full(g1), full(w_gate),
                  full(w_a), full(w_b), full(w_out), full(g2)],
        out_specs=[row(d), row(d)],
        out_shape=[jax.ShapeDtypeStruct((t, d), F32), jax.ShapeDtypeStruct((t, d), BF16)],
        compiler_params=pltpu.CompilerParams(dimension_semantics=("arbitrary",),
                                             vmem_limit_bytes=_vmem_limit(est)),
        name="merge",
    )(x2, oa, ob, g1, w_gate, w_a, w_b, w_out, g2)


def _ffn_kernel(h2_ref, x1_ref, wu_ref, cw_ref, cb_ref, wd_ref, o_ref, ubuf_ref, tail_ref, g_ref):
    tm = h2_ref.shape[0]
    pad = V7X_SUBLANES

    @pl.when(pl.program_id(1) == 0)
    def _():
        tail_ref[...] = jnp.zeros_like(tail_ref)

    h2 = h2_ref[...]
    inv_sqrt2 = float(1.0 / np.sqrt(2.0))
    for c in range(D_FF // FF_TILE):
        cs = slice(c * FF_TILE, (c + 1) * FF_TILE)
        u = jnp.dot(h2, wu_ref[:, cs], preferred_element_type=F32)
        v = jnp.dot(h2, wu_ref[:, D_FF + c * FF_TILE:D_FF + (c + 1) * FF_TILE], preferred_element_type=F32)
        ubuf_ref[0:pad, :] = tail_ref[c]
        ubuf_ref[pad:pad + tm, :] = u
        tail_ref[c] = u[tm - pad:tm, :]
        conv = (cb_ref[:, cs]
                + ubuf_ref[pad - 2:pad - 2 + tm, :] * cw_ref[0:1, cs]
                + ubuf_ref[pad - 1:pad - 1 + tm, :] * cw_ref[1:2, cs]
                + u * cw_ref[2:3, cs])
        gelu = 0.5 * conv * (1.0 + lax.erf(conv * inv_sqrt2))
        g_ref[:, cs] = (gelu * v).astype(BF16)
    o_ref[...] = x1_ref[...] + jnp.dot(g_ref[...], wd_ref[...], preferred_element_type=F32)


def _ffn(h2, x1, w_up, conv_w, conv_b, w_down, batch, seq):
    t, d = x1.shape
    tm = ROW_TILE
    n_s = seq // tm
    row = lambda n: pl.BlockSpec((tm, n), lambda b, s: (b * n_s + s, 0))
    full = _resident
    est = (2 * tm * d * (2 + 4 + 4) + 2 * (d * 2 * D_FF + D_FF * d)
           + (tm + 8) * FF_TILE * 4 + tm * D_FF * 2 + 8 * D_FF * 4 + 4 * tm * FF_TILE * 4)
    return pl.pallas_call(
        _ffn_kernel,
        grid=(batch, n_s),
        in_specs=[row(d), row(d), full(w_up), full(conv_w), full(conv_b), full(w_down)],
        out_specs=row(d),
        out_shape=jax.ShapeDtypeStruct((t, d), F32),
        scratch_shapes=[pltpu.VMEM((tm + V7X_SUBLANES, FF_TILE), F32),
                        pltpu.VMEM((D_FF // FF_TILE, V7X_SUBLANES, FF_TILE), F32),
                        pltpu.VMEM((tm, D_FF), BF16)],
        compiler_params=pltpu.CompilerParams(dimension_semantics=("arbitrary",) * 2,
                                             vmem_limit_bytes=_vmem_limit(est)),
        name="ffn",
    )(h2, x1, w_up, conv_w, conv_b, w_down)


def kernel(x, norm1_g, w_in, hg_lb_logits, hg_onorm_g, q_norm_g, k_norm_g, w_a, w_b, w_out,
           norm2_g, w_up, conv_w, conv_b, w_down):
    batch, seq, d = x.shape
    depth = w_in.shape[0]
    n_mix = 4 * HG_WIDTH + 3 * MB_WIDTH
    x2 = x.reshape(batch * seq, d)
    for l in range(depth):
        g1 = norm1_g[l][None, :]
        w_l = w_in[l].astype(BF16)
        qa, logf, kk, vh, gh, qb, kb, vb = _inproj(
            x2, g1, w_l[:, :n_mix], hg_lb_logits, q_norm_g[l], k_norm_g[l], seq, l)
        oa = _hgrn(qa, logf, kk, vh, gh, hg_onorm_g[l], batch, seq)
        ob = _moba(qb, kb, vb, batch, seq)
        x1, h2 = _merge(x2, oa, ob, g1, w_l[:, n_mix:], w_a[l].astype(BF16), w_b[l].astype(BF16),
                        w_out[l].astype(BF16), norm2_g[l][None, :])
        x2 = _ffn(h2, x1, w_up[l].astype(BF16), conv_w[l], conv_b[l][None, :],
                  w_down[l].astype(BF16), batch, seq)
    return x2.reshape(batch, seq, d)
```

```python
import functools

import numpy as np
import jax
import jax.numpy as jnp
from jax import lax
from jax.experimental import pallas as pl
from jax.experimental.pallas import tpu as pltpu

F32 = jnp.float32
BF16 = jnp.bfloat16

HG_HEADS = 4
HG_DIM = 128
HG_WIDTH = HG_HEADS * HG_DIM
HG_CHUNK = 64
MB_HEADS = 8
MB_HEAD_DIM = 64
MB_WIDTH = MB_HEADS * MB_HEAD_DIM
MB_BLOCK = 256
MB_TOPK = 3
ROPE_THETA = 10000.0
D_FF = 2816
CONV_WIDTH = 3
NORM_EPS = 1e-6

V7X_LANES = 128
V7X_SUBLANES = 8
V7X_BF16_ROWS = 16
V7X_VMEM_BYTES = 64 * 1024 * 1024

HG_SUB = 16
ROW_TILE = 1024
FF_TILE = 128
NEG_INF = float("-inf")

MB_GROUPS = MB_WIDTH // V7X_LANES
MB_VROWS = MB_HEAD_DIM + V7X_BF16_ROWS

_NT = (((1,), (1,)), ((), ()))
_TN = (((0,), (0,)), ((), ()))


def _vmem_limit(nbytes):
    return int(min(nbytes * 3 // 2 + (4 << 20), V7X_VMEM_BYTES - (4 << 20)))


def _sigmoid(x):
    return 0.5 * jnp.tanh(0.5 * x) + 0.5


def _rms_norm(x, g):
    ms = jnp.mean(x * x, axis=-1, keepdims=True)
    return x * lax.rsqrt(ms + NORM_EPS) * g


def _resident(a):
    return pl.BlockSpec(a.shape, lambda *_: (0,) * a.ndim, pipeline_mode=pl.Buffered(1))


def _store_groups(ref, val):
    for gi in range(ref.shape[0]):
        ref[gi] = val[:, gi * V7X_LANES:(gi + 1) * V7X_LANES].astype(ref.dtype)


def _load_groups(ref):
    return jnp.concatenate([ref[gi] for gi in range(ref.shape[0])], axis=1)


def _inproj_kernel(x_ref, g1_ref, w_ref, lbl_ref, qg_ref, kg_ref, rc_ref, rs_ref, ones_ref,
                   qa_ref, logf_ref, kk_ref, vh_ref, gh_ref, qb_ref, kb_ref, vb_ref, *, layer):
    h = _rms_norm(x_ref[...], g1_ref[...]).astype(BF16)

    def proj(c0, width):
        return jnp.dot(h, w_ref[:, c0:c0 + width], preferred_element_type=F32)

    w = HG_WIDTH
    p = proj(0, w)
    _store_groups(qa_ref, p * _sigmoid(p))
    a = lbl_ref[...]
    e = jnp.exp(a - jnp.max(a, axis=0, keepdims=True))
    lb = jnp.sum(e[0:layer + 1, :], axis=0, keepdims=True) / jnp.sum(e, axis=0, keepdims=True)
    f = lb + (1.0 - lb) * _sigmoid(proj(w, w))
    _store_groups(logf_ref, jnp.log2(f))
    _store_groups(kk_ref, 1.0 - f)
    _store_groups(vh_ref, proj(2 * w, w))
    p = proj(3 * w, w)
    _store_groups(gh_ref, p * _sigmoid(p))

    lane = lax.broadcasted_iota(jnp.int32, (1, V7X_LANES), 1)
    first_half = (lane % MB_HEAD_DIM) < (MB_HEAD_DIM // 2)
    rc = rc_ref[...]
    rs = rs_ref[...]
    ones_bd = ones_ref[...]

    def norm_rope(p, g_ref, out_ref, scale):
        for c in range(MB_GROUPS):
            sl = slice(c * V7X_LANES, (c + 1) * V7X_LANES)
            pc = p[:, sl]
            ss = jnp.dot((pc * pc).astype(BF16), ones_bd, preferred_element_type=F32)
            y = pc * lax.rsqrt(ss * (1.0 / MB_HEAD_DIM) + NORM_EPS) * g_ref[:, sl]
            partner = jnp.where(first_half,
                                pltpu.roll(y, V7X_LANES - MB_HEAD_DIM // 2, 1),
                                pltpu.roll(y, MB_HEAD_DIM // 2, 1))
            out_ref[c] = ((y * rc + partner * rs) * scale).astype(BF16)

    base = 4 * w
    norm_rope(proj(base, MB_WIDTH), qg_ref, qb_ref, float(np.log2(np.e) / np.sqrt(MB_HEAD_DIM)))
    norm_rope(proj(base + MB_WIDTH, MB_WIDTH), kg_ref, kb_ref, 1.0)
    _store_groups(vb_ref, proj(base + 2 * MB_WIDTH, MB_WIDTH))


def _rope_tables(seq):
    half = MB_HEAD_DIM // 2
    inv = 1.0 / (ROPE_THETA ** (jnp.arange(half, dtype=F32) * 2.0 / MB_HEAD_DIM))
    ang = jnp.arange(seq).astype(F32)[:, None] * inv[None, :]
    cos = jnp.cos(ang)
    sin = jnp.sin(ang)
    rc = jnp.concatenate([cos, cos, cos, cos], axis=-1)
    rs = jnp.concatenate([-sin, sin, -sin, sin], axis=-1)
    return rc, rs


def _inproj(x2, g1, w_mix, lb_logits, qg, kg, seq, layer):
    t, d = x2.shape
    d_in = w_mix.shape[1]
    tm = ROW_TILE
    rc, rs = _rope_tables(seq)
    blk = np.arange(V7X_LANES) // MB_HEAD_DIM
    ones_bd = jnp.asarray(blk[:, None] == blk[None, :], dtype=BF16)
    qg_t = jnp.tile(qg.astype(F32), MB_HEADS)[None, :]
    kg_t = jnp.tile(kg.astype(F32), MB_HEADS)[None, :]
    n_pos = seq // tm

    full = _resident
    grouped = lambda n, dt: jax.ShapeDtypeStruct((n // V7X_LANES, t, V7X_LANES), dt)
    outs = [grouped(HG_WIDTH, BF16),
            grouped(HG_WIDTH, F32),
            grouped(HG_WIDTH, BF16),
            grouped(HG_WIDTH, BF16),
            grouped(HG_WIDTH, BF16),
            grouped(MB_WIDTH, BF16),
            grouped(MB_WIDTH, BF16),
            grouped(MB_WIDTH, BF16)]
    est = (2 * tm * d * 4 + d * d_in * 2 + 2 * tm * (d_in * 2 + HG_WIDTH * 4)
           + 4 * tm * V7X_LANES * 4 + 4 * tm * HG_WIDTH * 4)
    return pl.pallas_call(
        functools.partial(_inproj_kernel, layer=layer),
        grid=(t // tm,),
        in_specs=[pl.BlockSpec((tm, d), lambda i: (i, 0)),
                  full(g1), full(w_mix), full(lb_logits), full(qg_t), full(kg_t),
                  pl.BlockSpec((tm, V7X_LANES), lambda i: (i % n_pos, 0)),
                  pl.BlockSpec((tm, V7X_LANES), lambda i: (i % n_pos, 0)),
                  full(ones_bd)],
        out_specs=[pl.BlockSpec((o.shape[0], tm, V7X_LANES), lambda i: (0, i, 0)) for o in outs],
        out_shape=outs,
        compiler_params=pltpu.CompilerParams(dimension_semantics=("arbitrary",),
                                             vmem_limit_bytes=_vmem_limit(est)),
        name="inproj",
    )(x2, g1, w_mix, lb_logits, qg_t, kg_t, rc, rs, ones_bd)


def _moba_kernel(q_ref, k_ref, v_ref, o_ref, vt_ref, km_ref):
    groups, nb, bk, _ = k_ref.shape
    bq = q_ref.shape[2]
    hd = MB_HEAD_DIM
    heads = V7X_LANES // hd
    n_sel = max(1, min(MB_TOPK, nb - 1))

    ones = jnp.ones((V7X_BF16_ROWS, bk), BF16)
    for gp in range(groups):
        for j in range(nb):
            v_t = v_ref[gp, j].astype(F32).T.astype(BF16)
            for h in range(heads):
                vt_ref[gp, h * MB_VROWS:h * MB_VROWS + hd, j * bk:(j + 1) * bk] = v_t[h * hd:(h + 1) * hd, :]
                vt_ref[gp, h * MB_VROWS + hd:(h + 1) * MB_VROWS, j * bk:(j + 1) * bk] = ones
            km_ref[gp, j:j + 1, :] = jnp.mean(k_ref[gp, j].astype(F32), axis=0, keepdims=True)

    lane = lax.broadcasted_iota(jnp.int32, (1, V7X_LANES), 1)
    causal = (lax.broadcasted_iota(jnp.int32, (bk, bq), 0) <= lax.broadcasted_iota(jnp.int32, (bk, bq), 1))
    blk = lax.broadcasted_iota(jnp.int32, (nb, bq), 0)
    live = {}

    def scores(gp, qi):
        q = q_ref[gp, qi]
        for h in range(heads):
            qh = jnp.where((lane // hd) == h, q, jnp.zeros_like(q))
            bias = None
            if qi > n_sel:
                g = lax.dot_general(km_ref[gp].astype(BF16), qh, _NT, preferred_element_type=F32)
                cnt = jnp.zeros((nb, bq), F32)
                for jp in range(qi):
                    gj = g[jp:jp + 1, :]
                    cnt = cnt + ((gj > g) | ((gj == g) & (jp < blk))).astype(F32)
                bias = jnp.where((cnt < float(n_sel)) & (blk < qi), 0.0, NEG_INF).astype(BF16)
            s_list = []
            for j in range(qi):
                s = lax.dot_general(k_ref[gp, j], qh, _NT, preferred_element_type=F32).astype(BF16)
                s_list.append(s if bias is None else s + bias[j:j + 1, :])
            s_own = lax.dot_general(k_ref[gp, qi], qh, _NT, preferred_element_type=F32).astype(BF16)
            s_list.append(jnp.where(causal, s_own, jnp.full_like(s_own, NEG_INF)))
            live[gp, qi, h] = s_list

    def numerators(gp, qi):
        for h in range(heads):
            s_list = live[gp, qi, h]
            m = jnp.max(functools.reduce(jnp.maximum, s_list), axis=0, keepdims=True)
            live[gp, qi, h] = jnp.concatenate([jnp.exp2(s - m) for s in s_list], axis=0)

    def outputs(gp, qi):
        outs = []
        for h in range(heads):
            acc = jnp.dot(vt_ref[gp, h * MB_VROWS:(h + 1) * MB_VROWS, 0:(qi + 1) * bk], live.pop((gp, qi, h)),
                          preferred_element_type=F32)
            outs.append(acc[0:hd, :] * (1.0 / acc[hd:hd + 1, :]))
        o_ref[gp, qi] = jnp.concatenate(outs, axis=0).T.astype(BF16)

    q_order = list(range(nb - 1, -1, -1))
    for step in range(nb + 2):
        for gp in range(groups):
            if step < nb:
                scores(gp, q_order[step])
            if 0 <= step - 1 < nb:
                numerators(gp, q_order[step - 1])
            if 0 <= step - 2 < nb:
                outputs(gp, q_order[step - 2])


def _moba(qb, kb, vb, batch, seq):
    nb = seq // MB_BLOCK
    r5 = lambda a: a.reshape(MB_GROUPS, batch, nb, MB_BLOCK, V7X_LANES)
    spec = pl.BlockSpec((MB_GROUPS, None, nb, MB_BLOCK, V7X_LANES), lambda b: (0, b, 0, 0, 0))
    est = MB_GROUPS * (2 * 4 * seq * V7X_LANES * 2 + 2 * MB_VROWS * seq * 2 + 4 * seq * MB_BLOCK * 4)
    out = pl.pallas_call(
        _moba_kernel,
        grid=(batch,),
        in_specs=[spec, spec, spec],
        out_specs=spec,
        out_shape=jax.ShapeDtypeStruct((MB_GROUPS, batch, nb, MB_BLOCK, V7X_LANES), BF16),
        scratch_shapes=[pltpu.VMEM((MB_GROUPS, 2 * MB_VROWS, seq), BF16),
                        pltpu.VMEM((MB_GROUPS, nb, V7X_LANES), F32)],
        compiler_params=pltpu.CompilerParams(dimension_semantics=("arbitrary",),
                                             vmem_limit_bytes=_vmem_limit(est)),
        name="moba",
    )(r5(qb), r5(kb), r5(vb))
    return out.reshape(MB_GROUPS, batch * seq, V7X_LANES)


def _hgrn_kernel(qa_ref, logf_ref, kk_ref, v_ref, gate_ref, og_ref, o_ref):
    heads, seq, _ = qa_ref.shape
    ch = HG_CHUNK
    causal = (lax.broadcasted_iota(jnp.int32, (ch, ch), 0) >= lax.broadcasted_iota(jnp.int32, (ch, ch), 1))
    tri = causal.astype(BF16)
    chunks = [slice(g * ch, (g + 1) * ch) for g in range(seq // ch)]
    units = [(hh, cs) for hh in range(heads) for cs in chunks]

    lf = [logf_ref[hh] for hh in range(heads)]
    hi = [x.astype(BF16) for x in lf]
    lo = [(x - h.astype(F32)).astype(BF16) for x, h in zip(lf, hi)]
    qa_h = [qa_ref[hh] for hh in range(heads)]
    kk_h = [kk_ref[hh] for hh in range(heads)]
    v_h = [v_ref[hh] for hh in range(heads)]
    cums = [jnp.dot(tri, hi[hh][cs], preferred_element_type=F32)
            + jnp.dot(tri, lo[hh][cs], preferred_element_type=F32) for hh, cs in units]
    ops = []
    for (hh, cs), cum in zip(units, cums):
        qa, kk = qa_h[hh][cs], kk_h[hh][cs]
        last = cum[ch - 1:ch, :]
        qc = qa * jnp.exp2(cum).astype(BF16)
        kbar = kk * jnp.exp2(last - cum).astype(BF16)
        qts, kts = [], []
        for a in range(ch // HG_SUB):
            lo_r, hi_r = a * HG_SUB, (a + 1) * HG_SUB
            rho = cum[lo_r + HG_SUB // 2 - 1:lo_r + HG_SUB // 2, :]
            qts.append(qa[lo_r:hi_r] * jnp.exp2(cum[lo_r:hi_r] - rho).astype(BF16))
            kt = kk[0:hi_r] * jnp.exp2(rho - cum[0:hi_r]).astype(BF16)
            if hi_r < ch:
                kt = jnp.concatenate([kt, jnp.zeros((ch - hi_r, HG_DIM), BF16)], axis=0)
            kts.append(kt)
        ops.append((qc, kbar, qts, kts, jnp.exp2(last)))
    attns = []
    for qc, kbar, qts, kts, decay in ops:
        a_rows = [lax.dot_general(qt, kt, _NT, preferred_element_type=F32) for qt, kt in zip(qts, kts)]
        attns.append(jnp.where(causal, jnp.concatenate(a_rows, axis=0), 0.0).astype(BF16))
    o_intra = [jnp.dot(attn, v_h[hh][cs], preferred_element_type=F32) for attn, (hh, cs) in zip(attns, units)]
    incs = [lax.dot_general(v_h[hh][cs], op[1], _TN, preferred_element_type=F32) for op, (hh, cs) in zip(ops, units)]
    state = [jnp.zeros((HG_DIM, HG_DIM), F32) for _ in range(heads)]
    for g, cs in enumerate(chunks):
        for hh in range(heads):
            i = hh * len(chunks) + g
            qc, decay = ops[i][0], ops[i][4]
            o = o_intra[i] + lax.dot_general(qc, state[hh].astype(BF16), _NT, preferred_element_type=F32)
            state[hh] = state[hh] * decay + incs[i]
            o_ref[hh, cs, :] = (_rms_norm(o, og_ref[hh]) * gate_ref[hh, cs, :].astype(F32)).astype(BF16)


def _hgrn(qa, logf, kk, vh, gh, og, batch, seq):
    assert seq % HG_CHUNK == 0
    r4 = lambda a: a.reshape(HG_HEADS, batch, seq, HG_DIM)
    spec = pl.BlockSpec((HG_HEADS, None, seq, HG_DIM), lambda b: (0, b, 0, 0))
    og3 = og.reshape(HG_HEADS, 1, HG_DIM)
    est = 2 * HG_HEADS * seq * HG_DIM * (2 * 5 + 4) + 8 * HG_HEADS * seq * HG_DIM * 4
    out = pl.pallas_call(
        _hgrn_kernel,
        grid=(batch,),
        in_specs=[spec, spec, spec, spec, spec, _resident(og3)],
        out_specs=spec,
        out_shape=jax.ShapeDtypeStruct((HG_HEADS, batch, seq, HG_DIM), BF16),
        compiler_params=pltpu.CompilerParams(dimension_semantics=("arbitrary",),
                                             vmem_limit_bytes=_vmem_limit(est)),
        name="hgrn",
    )(r4(qa), r4(logf), r4(kk), r4(vh), r4(gh), og3)
    return out.reshape(HG_HEADS, batch * seq, HG_DIM)


def _merge_kernel(x_ref, oa_ref, ob_ref, g1_ref, wg_ref, wa_ref, wb_ref, wo_ref, g2_ref, x1_ref, h2_ref):
    x = x_ref[...]
    d = x.shape[1]
    ma = jnp.dot(_load_groups(oa_ref), wa_ref[...], preferred_element_type=F32)
    mb = jnp.dot(_load_groups(ob_ref), wb_ref[...], preferred_element_type=F32)
    h1 = _rms_norm(x, g1_ref[...]).astype(BF16)
    ga = jnp.dot(h1, wg_ref[:, 0:d], preferred_element_type=F32)
    gb = jnp.dot(h1, wg_ref[:, d:2 * d], preferred_element_type=F32)
    mix = _sigmoid(ga) * ma + _sigmoid(gb) * mb
    x1 = x + jnp.dot(mix.astype(BF16), wo_ref[...], preferred_element_type=F32)
    x1_ref[...] = x1
    h2_ref[...] = _rms_norm(x1, g2_ref[...]).astype(BF16)


def _merge(x2, oa, ob, g1, w_gate, w_a, w_b, w_out, g2):
    t, d = x2.shape
    tm = ROW_TILE
    row = lambda n: pl.BlockSpec((tm, n), lambda i: (i, 0))
    grouped = lambda a: pl.BlockSpec((a.shape[0], tm, V7X_LANES), lambda i: (0, i, 0))
    full = _resident
    est = (2 * tm * d * (4 + 4 + 2) + 4 * tm * HG_WIDTH * 2
           + 2 * (2 * d * d + 2 * HG_WIDTH * d + d * d) + 4 * tm * d * 4)
    return pl.pallas_call(
        _merge_kernel,
        grid=(t // tm,),
        in_specs=[row(d), grouped(oa), grouped(ob), full(g1), full(w_gate),
                  full(w_a), full(w_b), full(w_out), full(g2)],
        out_specs=[row(d), row(d)],
        out_shape=[jax.ShapeDtypeStruct((t, d), F32), jax.ShapeDtypeStruct((t, d), BF16)],
        compiler_params=pltpu.CompilerParams(dimension_semantics=("arbitrary",),
                                             vmem_limit_bytes=_vmem_limit(est)),
        name="merge",
    )(x2, oa, ob, g1, w_gate, w_a, w_b, w_out, g2)


def _ffn_kernel(h2_ref, x1_ref, wu_ref, cw_ref, cb_ref, wd_ref, o_ref, ubuf_ref, tail_ref, g_ref):
    tm = h2_ref.shape[0]
    pad = V7X_SUBLANES

    @pl.when(pl.program_id(1) == 0)
    def _():
        tail_ref[...] = jnp.zeros_like(tail_ref)

    h2 = h2_ref[...]
    inv_sqrt2 = float(1.0 / np.sqrt(2.0))
    for c in range(D_FF // FF_TILE):
        cs = slice(c * FF_TILE, (c + 1) * FF_TILE)
        w_uv = jnp.concatenate([wu_ref[:, cs], wu_ref[:, D_FF + c * FF_TILE:D_FF + (c + 1) * FF_TILE]], axis=1)
        uv = jnp.dot(h2, w_uv, preferred_element_type=F32)
        u, v = uv[:, 0:FF_TILE], uv[:, FF_TILE:2 * FF_TILE]
        ubuf_ref[0:pad, :] = tail_ref[c]
        ubuf_ref[pad:pad + tm, :] = u
        tail_ref[c] = u[tm - pad:tm, :]
        conv = (cb_ref[:, cs]
                + ubuf_ref[pad - 2:pad - 2 + tm, :] * cw_ref[0:1, cs]
                + ubuf_ref[pad - 1:pad - 1 + tm, :] * cw_ref[1:2, cs]
                + u * cw_ref[2:3, cs])
        gelu = 0.5 * conv * (1.0 + lax.erf(conv * inv_sqrt2))
        g_ref[:, cs] = (gelu * v).astype(BF16)
    o_ref[...] = x1_ref[...] + jnp.dot(g_ref[...], wd_ref[...], preferred_element_type=F32)


def _ffn(h2, x1, w_up, conv_w, conv_b, w_down, batch, seq):
    t, d = x1.shape
    tm = ROW_TILE
    n_s = seq // tm
    row = lambda n: pl.BlockSpec((tm, n), lambda b, s: (b * n_s + s, 0))
    full = _resident
    est = (2 * tm * d * (2 + 4 + 4) + 2 * (d * 2 * D_FF + D_FF * d)
           + (tm + 8) * FF_TILE * 4 + tm * D_FF * 2 + 8 * D_FF * 4 + 4 * tm * FF_TILE * 4)
    return pl.pallas_call(
        _ffn_kernel,
        grid=(batch, n_s),
        in_specs=[row(d), row(d), full(w_up), full(conv_w), full(conv_b), full(w_down)],
        out_specs=row(d),
        out_shape=jax.ShapeDtypeStruct((t, d), F32),
        scratch_shapes=[pltpu.VMEM((tm + V7X_SUBLANES, FF_TILE), F32),
                        pltpu.VMEM((D_FF // FF_TILE, V7X_SUBLANES, FF_TILE), F32),
                        pltpu.VMEM((tm, D_FF), BF16)],
        compiler_params=pltpu.CompilerParams(dimension_semantics=("arbitrary",) * 2,
                                             vmem_limit_bytes=_vmem_limit(est)),
        name="ffn",
    )(h2, x1, w_up, conv_w, conv_b, w_down)


def kernel(x, norm1_g, w_in, hg_lb_logits, hg_onorm_g, q_norm_g, k_norm_g, w_a, w_b, w_out,
           norm2_g, w_up, conv_w, conv_b, w_down):
    batch, seq, d = x.shape
    depth = w_in.shape[0]
    n_mix = 4 * HG_WIDTH + 3 * MB_WIDTH
    x2 = x.reshape(batch * seq, d)
    for l in range(depth):
        g1 = norm1_g[l][None, :]
        w_mix = w_in[l][:, :n_mix].astype(BF16)
        w_gate = w_in[l][:, n_mix:].astype(BF16)
        qa, logf, kk, vh, gh, qb, kb, vb = _inproj(
            x2, g1, w_mix, hg_lb_logits, q_norm_g[l], k_norm_g[l], seq, l)
        oa = _hgrn(qa, logf, kk, vh, gh, hg_onorm_g[l], batch, seq)
        ob = _moba(qb, kb, vb, batch, seq)
        x1, h2 = _merge(x2, oa, ob, g1, w_gate, w_a[l].astype(BF16), w_b[l].astype(BF16),
                        w_out[l].astype(BF16), norm2_g[l][None, :])
        x2 = _ffn(h2, x1, w_up[l].astype(BF16), conv_w[l], conv_b[l][None, :],
                  w_down[l].astype(BF16), batch, seq)
    return x2.reshape(batch, seq, d)
```

```python
import functools

import numpy as np
import jax
import jax.numpy as jnp
from jax import lax
from jax.experimental import pallas as pl
from jax.experimental.pallas import tpu as pltpu

F32 = jnp.float32
BF16 = jnp.bfloat16

HG_HEADS = 4
HG_DIM = 128
HG_WIDTH = HG_HEADS * HG_DIM
HG_CHUNK = 64
MB_HEADS = 8
MB_HEAD_DIM = 64
MB_WIDTH = MB_HEADS * MB_HEAD_DIM
MB_BLOCK = 256
MB_TOPK = 3
ROPE_THETA = 10000.0
D_FF = 2816
CONV_WIDTH = 3
NORM_EPS = 1e-6

V7X_LANES = 128
V7X_SUBLANES = 8
V7X_BF16_ROWS = 16
V7X_VMEM_BYTES = 64 * 1024 * 1024

HG_SUB = 16
ROW_TILE = 1024
FF_TILE = 256
NEG_INF = float("-inf")

MB_GROUPS = MB_WIDTH // V7X_LANES
MB_VROWS = MB_HEAD_DIM + V7X_BF16_ROWS

_NT = (((1,), (1,)), ((), ()))
_TN = (((0,), (0,)), ((), ()))


def _vmem_limit(nbytes):
    return int(min(nbytes * 3 // 2 + (4 << 20), V7X_VMEM_BYTES - (4 << 20)))


def _sigmoid(x):
    return 0.5 * jnp.tanh(0.5 * x) + 0.5


def _rms_norm(x, g):
    ms = jnp.mean(x * x, axis=-1, keepdims=True)
    return x * lax.rsqrt(ms + NORM_EPS) * g


def _resident(a):
    return pl.BlockSpec(a.shape, lambda *_: (0,) * a.ndim, pipeline_mode=pl.Buffered(1))


def _store_groups(ref, val):
    for gi in range(ref.shape[0]):
        ref[gi] = val[:, gi * V7X_LANES:(gi + 1) * V7X_LANES].astype(ref.dtype)


def _load_groups(ref):
    return jnp.concatenate([ref[gi] for gi in range(ref.shape[0])], axis=1)


def _inproj_kernel(x_ref, g1_ref, w_ref, lbl_ref, qg_ref, kg_ref, rc_ref, rs_ref, ones_ref,
                   qa_ref, logf_ref, kk_ref, vh_ref, gh_ref, qb_ref, kb_ref, vb_ref, *, layer):
    h = _rms_norm(x_ref[...], g1_ref[...]).astype(BF16)

    def proj(c0, width):
        return jnp.dot(h, w_ref[:, c0:c0 + width], preferred_element_type=F32)

    w = HG_WIDTH
    p = proj(0, w)
    _store_groups(qa_ref, p * _sigmoid(p))
    a = lbl_ref[...]
    e = jnp.exp(a - jnp.max(a, axis=0, keepdims=True))
    lb = jnp.sum(e[0:layer + 1, :], axis=0, keepdims=True) / jnp.sum(e, axis=0, keepdims=True)
    f = lb + (1.0 - lb) * _sigmoid(proj(w, w))
    _store_groups(logf_ref, jnp.log2(f))
    _store_groups(kk_ref, 1.0 - f)
    _store_groups(vh_ref, proj(2 * w, w))
    p = proj(3 * w, w)
    _store_groups(gh_ref, p * _sigmoid(p))

    lane = lax.broadcasted_iota(jnp.int32, (1, V7X_LANES), 1)
    first_half = (lane % MB_HEAD_DIM) < (MB_HEAD_DIM // 2)
    rc = rc_ref[...]
    rs = rs_ref[...]
    ones_bd = ones_ref[...]

    def norm_rope(p, g_ref, out_ref, scale):
        for c in range(MB_GROUPS):
            sl = slice(c * V7X_LANES, (c + 1) * V7X_LANES)
            pc = p[:, sl]
            ss = jnp.dot((pc * pc).astype(BF16), ones_bd, preferred_element_type=F32)
            y = pc * lax.rsqrt(ss * (1.0 / MB_HEAD_DIM) + NORM_EPS) * g_ref[:, sl]
            partner = jnp.where(first_half,
                                pltpu.roll(y, V7X_LANES - MB_HEAD_DIM // 2, 1),
                                pltpu.roll(y, MB_HEAD_DIM // 2, 1))
            out_ref[c] = ((y * rc + partner * rs) * scale).astype(BF16)

    base = 4 * w
    norm_rope(proj(base, MB_WIDTH), qg_ref, qb_ref, float(np.log2(np.e) / np.sqrt(MB_HEAD_DIM)))
    norm_rope(proj(base + MB_WIDTH, MB_WIDTH), kg_ref, kb_ref, 1.0)
    _store_groups(vb_ref, proj(base + 2 * MB_WIDTH, MB_WIDTH))


def _rope_tables(seq):
    half = MB_HEAD_DIM // 2
    inv = 1.0 / (ROPE_THETA ** (jnp.arange(half, dtype=F32) * 2.0 / MB_HEAD_DIM))
    ang = jnp.arange(seq).astype(F32)[:, None] * inv[None, :]
    cos = jnp.cos(ang)
    sin = jnp.sin(ang)
    rc = jnp.concatenate([cos, cos, cos, cos], axis=-1)
    rs = jnp.concatenate([-sin, sin, -sin, sin], axis=-1)
    return rc, rs


def _inproj(x2, g1, w_mix, lb_logits, qg, kg, seq, layer):
    t, d = x2.shape
    d_in = w_mix.shape[1]
    tm = ROW_TILE
    rc, rs = _rope_tables(seq)
    blk = np.arange(V7X_LANES) // MB_HEAD_DIM
    ones_bd = jnp.asarray(blk[:, None] == blk[None, :], dtype=BF16)
    qg_t = jnp.tile(qg.astype(F32), MB_HEADS)[None, :]
    kg_t = jnp.tile(kg.astype(F32), MB_HEADS)[None, :]
    n_pos = seq // tm

    full = _resident
    grouped = lambda n, dt: jax.ShapeDtypeStruct((n // V7X_LANES, t, V7X_LANES), dt)
    outs = [grouped(HG_WIDTH, BF16),
            grouped(HG_WIDTH, F32),
            grouped(HG_WIDTH, BF16),
            grouped(HG_WIDTH, BF16),
            grouped(HG_WIDTH, BF16),
            grouped(MB_WIDTH, BF16),
            grouped(MB_WIDTH, BF16),
            grouped(MB_WIDTH, BF16)]
    est = (2 * tm * d * 4 + d * d_in * 2 + 2 * tm * (d_in * 2 + HG_WIDTH * 4)
           + 4 * tm * V7X_LANES * 4 + 4 * tm * HG_WIDTH * 4)
    return pl.pallas_call(
        functools.partial(_inproj_kernel, layer=layer),
        grid=(t // tm,),
        in_specs=[pl.BlockSpec((tm, d), lambda i: (i, 0)),
                  full(g1), full(w_mix), full(lb_logits), full(qg_t), full(kg_t),
                  pl.BlockSpec((tm, V7X_LANES), lambda i: (i % n_pos, 0)),
                  pl.BlockSpec((tm, V7X_LANES), lambda i: (i % n_pos, 0)),
                  full(ones_bd)],
        out_specs=[pl.BlockSpec((o.shape[0], tm, V7X_LANES), lambda i: (0, i, 0)) for o in outs],
        out_shape=outs,
        compiler_params=pltpu.CompilerParams(dimension_semantics=("arbitrary",),
                                             vmem_limit_bytes=_vmem_limit(est)),
        name="inproj",
    )(x2, g1, w_mix, lb_logits, qg_t, kg_t, rc, rs, ones_bd)


def _moba_kernel(q_ref, k_ref, v_ref, o_ref, vt_ref, km_ref):
    groups, nb, bk, _ = k_ref.shape
    bq = q_ref.shape[2]
    hd = MB_HEAD_DIM
    heads = V7X_LANES // hd
    n_sel = max(1, min(MB_TOPK, nb - 1))

    ones = jnp.ones((V7X_BF16_ROWS, bk), BF16)
    for gp in range(groups):
        for j in range(nb):
            v_t = v_ref[gp, j].astype(F32).T.astype(BF16)
            for h in range(heads):
                vt_ref[gp, h * MB_VROWS:h * MB_VROWS + hd, j * bk:(j + 1) * bk] = v_t[h * hd:(h + 1) * hd, :]
                vt_ref[gp, h * MB_VROWS + hd:(h + 1) * MB_VROWS, j * bk:(j + 1) * bk] = ones
            km_ref[gp, j:j + 1, :] = jnp.mean(k_ref[gp, j].astype(F32), axis=0, keepdims=True)

    lane = lax.broadcasted_iota(jnp.int32, (1, V7X_LANES), 1)
    causal = (lax.broadcasted_iota(jnp.int32, (bk, bq), 0) <= lax.broadcasted_iota(jnp.int32, (bk, bq), 1))
    blk = lax.broadcasted_iota(jnp.int32, (nb, bq), 0)
    live = {}

    def scores(gp, qi):
        q = q_ref[gp, qi]
        for h in range(heads):
            qh = jnp.where((lane // hd) == h, q, jnp.zeros_like(q))
            bias = None
            if qi > n_sel:
                g = lax.dot_general(km_ref[gp].astype(BF16), qh, _NT, preferred_element_type=F32)
                cnt = jnp.zeros((nb, bq), F32)
                for jp in range(qi):
                    gj = g[jp:jp + 1, :]
                    cnt = cnt + ((gj > g) | ((gj == g) & (jp < blk))).astype(F32)
                bias = jnp.where((cnt < float(n_sel)) & (blk < qi), 0.0, NEG_INF).astype(BF16)
            s_list = []
            for j in range(qi):
                s = lax.dot_general(k_ref[gp, j], qh, _NT, preferred_element_type=F32).astype(BF16)
                s_list.append(s if bias is None else s + bias[j:j + 1, :])
            s_own = lax.dot_general(k_ref[gp, qi], qh, _NT, preferred_element_type=F32).astype(BF16)
            s_list.append(jnp.where(causal, s_own, jnp.full_like(s_own, NEG_INF)))
            live[gp, qi, h] = s_list

    def numerators(gp, qi):
        for h in range(heads):
            s_list = live[gp, qi, h]
            m = jnp.max(functools.reduce(jnp.maximum, s_list), axis=0, keepdims=True)
            live[gp, qi, h] = jnp.concatenate([jnp.exp2(s - m) for s in s_list], axis=0)

    def outputs(gp, qi):
        outs = []
        for h in range(heads):
            acc = jnp.dot(vt_ref[gp, h * MB_VROWS:(h + 1) * MB_VROWS, 0:(qi + 1) * bk], live.pop((gp, qi, h)),
                          preferred_element_type=F32)
            outs.append(acc[0:hd, :] * (1.0 / acc[hd:hd + 1, :]))
        o_ref[gp, qi] = jnp.concatenate(outs, axis=0).T.astype(BF16)

    q_order = list(range(nb - 1, -1, -1))
    for step in range(nb + 2):
        for gp in range(groups):
            if step < nb:
                scores(gp, q_order[step])
            if 0 <= step - 1 < nb:
                numerators(gp, q_order[step - 1])
            if 0 <= step - 2 < nb:
                outputs(gp, q_order[step - 2])


def _moba(qb, kb, vb, batch, seq):
    nb = seq // MB_BLOCK
    r5 = lambda a: a.reshape(MB_GROUPS, batch, nb, MB_BLOCK, V7X_LANES)
    spec = pl.BlockSpec((MB_GROUPS, None, nb, MB_BLOCK, V7X_LANES), lambda b: (0, b, 0, 0, 0))
    est = MB_GROUPS * (2 * 4 * seq * V7X_LANES * 2 + 2 * MB_VROWS * seq * 2 + 4 * seq * MB_BLOCK * 4)
    out = pl.pallas_call(
        _moba_kernel,
        grid=(batch,),
        in_specs=[spec, spec, spec],
        out_specs=spec,
        out_shape=jax.ShapeDtypeStruct((MB_GROUPS, batch, nb, MB_BLOCK, V7X_LANES), BF16),
        scratch_shapes=[pltpu.VMEM((MB_GROUPS, 2 * MB_VROWS, seq), BF16),
                        pltpu.VMEM((MB_GROUPS, nb, V7X_LANES), F32)],
        compiler_params=pltpu.CompilerParams(dimension_semantics=("arbitrary",),
                                             vmem_limit_bytes=_vmem_limit(est)),
        name="moba",
    )(r5(qb), r5(kb), r5(vb))
    return out.reshape(MB_GROUPS, batch * seq, V7X_LANES)


def _hgrn_kernel(qa_ref, logf_ref, kk_ref, v_ref, gate_ref, og_ref, o_ref):
    heads, seq, _ = qa_ref.shape
    ch = HG_CHUNK
    causal = (lax.broadcasted_iota(jnp.int32, (ch, ch), 0) >= lax.broadcasted_iota(jnp.int32, (ch, ch), 1))
    tri = causal.astype(BF16)
    chunks = [slice(g * ch, (g + 1) * ch) for g in range(seq // ch)]
    units = [(hh, cs) for hh in range(heads) for cs in chunks]

    lf = [logf_ref[hh] for hh in range(heads)]
    hi = [x.astype(BF16) for x in lf]
    lo = [(x - h.astype(F32)).astype(BF16) for x, h in zip(lf, hi)]
    qa_h = [qa_ref[hh] for hh in range(heads)]
    kk_h = [kk_ref[hh] for hh in range(heads)]
    v_h = [v_ref[hh] for hh in range(heads)]
    cums = [jnp.dot(tri, hi[hh][cs], preferred_element_type=F32)
            + jnp.dot(tri, lo[hh][cs], preferred_element_type=F32) for hh, cs in units]
    ops = []
    for (hh, cs), cum in zip(units, cums):
        qa, kk = qa_h[hh][cs], kk_h[hh][cs]
        last = cum[ch - 1:ch, :]
        qc = qa * jnp.exp2(cum).astype(BF16)
        kbar = kk * jnp.exp2(last - cum).astype(BF16)
        qts, kts = [], []
        for a in range(ch // HG_SUB):
            lo_r, hi_r = a * HG_SUB, (a + 1) * HG_SUB
            rho = cum[lo_r + HG_SUB // 2 - 1:lo_r + HG_SUB // 2, :]
            qts.append(qa[lo_r:hi_r] * jnp.exp2(cum[lo_r:hi_r] - rho).astype(BF16))
            kt = kk[0:hi_r] * jnp.exp2(rho - cum[0:hi_r]).astype(BF16)
            if hi_r < ch:
                kt = jnp.concatenate([kt, jnp.zeros((ch - hi_r, HG_DIM), BF16)], axis=0)
            kts.append(kt)
        ops.append((qc, kbar, qts, kts, jnp.exp2(last)))
    attns = []
    for qc, kbar, qts, kts, decay in ops:
        a_rows = [lax.dot_general(qt, kt, _NT, preferred_element_type=F32) for qt, kt in zip(qts, kts)]
        attns.append(jnp.where(causal, jnp.concatenate(a_rows, axis=0), 0.0).astype(BF16))
    o_intra = [jnp.dot(attn, v_h[hh][cs], preferred_element_type=F32) for attn, (hh, cs) in zip(attns, units)]
    incs = [lax.dot_general(v_h[hh][cs], op[1], _TN, preferred_element_type=F32) for op, (hh, cs) in zip(ops, units)]
    state = [jnp.zeros((HG_DIM, HG_DIM), F32) for _ in range(heads)]
    for g, cs in enumerate(chunks):
        for hh in range(heads):
            i = hh * len(chunks) + g
            qc, decay = ops[i][0], ops[i][4]
            o = o_intra[i] + lax.dot_general(qc, state[hh].astype(BF16), _NT, preferred_element_type=F32)
            state[hh] = state[hh] * decay + incs[i]
            o_ref[hh, cs, :] = (_rms_norm(o, og_ref[hh]) * gate_ref[hh, cs, :].astype(F32)).astype(BF16)


def _hgrn(qa, logf, kk, vh, gh, og, batch, seq):
    assert seq % HG_CHUNK == 0
    r4 = lambda a: a.reshape(HG_HEADS, batch, seq, HG_DIM)
    spec = pl.BlockSpec((HG_HEADS, None, seq, HG_DIM), lambda b: (0, b, 0, 0))
    og3 = og.reshape(HG_HEADS, 1, HG_DIM)
    est = 2 * HG_HEADS * seq * HG_DIM * (2 * 5 + 4) + 8 * HG_HEADS * seq * HG_DIM * 4
    out = pl.pallas_call(
        _hgrn_kernel,
        grid=(batch,),
        in_specs=[spec, spec, spec, spec, spec, _resident(og3)],
        out_specs=spec,
        out_shape=jax.ShapeDtypeStruct((HG_HEADS, batch, seq, HG_DIM), BF16),
        compiler_params=pltpu.CompilerParams(dimension_semantics=("arbitrary",),
                                             vmem_limit_bytes=_vmem_limit(est)),
        name="hgrn",
    )(r4(qa), r4(logf), r4(kk), r4(vh), r4(gh), og3)
    return out.reshape(HG_HEADS, batch * seq, HG_DIM)


def _merge_kernel(x_ref, oa_ref, ob_ref, g1_ref, wg_ref, wa_ref, wb_ref, wo_ref, g2_ref, x1_ref, h2_ref):
    x = x_ref[...]
    d = x.shape[1]
    ma = jnp.dot(_load_groups(oa_ref), wa_ref[...], preferred_element_type=F32)
    mb = jnp.dot(_load_groups(ob_ref), wb_ref[...], preferred_element_type=F32)
    h1 = _rms_norm(x, g1_ref[...]).astype(BF16)
    ga = jnp.dot(h1, wg_ref[:, 0:d], preferred_element_type=F32)
    gb = jnp.dot(h1, wg_ref[:, d:2 * d], preferred_element_type=F32)
    mix = _sigmoid(ga) * ma + _sigmoid(gb) * mb
    x1 = x + jnp.dot(mix.astype(BF16), wo_ref[...], preferred_element_type=F32)
    x1_ref[...] = x1
    h2_ref[...] = _rms_norm(x1, g2_ref[...]).astype(BF16)


def _merge(x2, oa, ob, g1, w_gate, w_a, w_b, w_out, g2):
    t, d = x2.shape
    tm = ROW_TILE
    row = lambda n: pl.BlockSpec((tm, n), lambda i: (i, 0))
    grouped = lambda a: pl.BlockSpec((a.shape[0], tm, V7X_LANES), lambda i: (0, i, 0))
    full = _resident
    est = (2 * tm * d * (4 + 4 + 2) + 4 * tm * HG_WIDTH * 2
           + 2 * (2 * d * d + 2 * HG_WIDTH * d + d * d) + 4 * tm * d * 4)
    return pl.pallas_call(
        _merge_kernel,
        grid=(t // tm,),
        in_specs=[row(d), grouped(oa), grouped(ob), full(g1), full(w_gate),
                  full(w_a), full(w_b), full(w_out), full(g2)],
        out_specs=[row(d), row(d)],
        out_shape=[jax.ShapeDtypeStruct((t, d), F32), jax.ShapeDtypeStruct((t, d), BF16)],
        compiler_params=pltpu.CompilerParams(dimension_semantics=("arbitrary",),
                                             vmem_limit_bytes=_vmem_limit(est)),
        name="merge",
    )(x2, oa, ob, g1, w_gate, w_a, w_b, w_out, g2)


def _ffn_kernel(h2_ref, x1_ref, wu_ref, cw_ref, cb_ref, wd_ref, o_ref, ubuf_ref, tail_ref, g_ref):
    tm = h2_ref.shape[0]
    pad = V7X_SUBLANES

    @pl.when(pl.program_id(1) == 0)
    def _():
        tail_ref[...] = jnp.zeros_like(tail_ref)

    h2 = h2_ref[...]
    inv_sqrt2 = float(1.0 / np.sqrt(2.0))
    for c in range(D_FF // FF_TILE):
        cs = slice(c * FF_TILE, (c + 1) * FF_TILE)
        u = jnp.dot(h2, wu_ref[:, cs], preferred_element_type=F32)
        v = jnp.dot(h2, wu_ref[:, D_FF + c * FF_TILE:D_FF + (c + 1) * FF_TILE], preferred_element_type=F32)
        ubuf_ref[0:pad, :] = tail_ref[c]
        ubuf_ref[pad:pad + tm, :] = u
        tail_ref[c] = u[tm - pad:tm, :]
        conv = (cb_ref[:, cs]
                + ubuf_ref[pad - 2:pad - 2 + tm, :] * cw_ref[0:1, cs]
                + ubuf_ref[pad - 1:pad - 1 + tm, :] * cw_ref[1:2, cs]
                + u * cw_ref[2:3, cs])
        gelu = 0.5 * conv * (1.0 + lax.erf(conv * inv_sqrt2))
        g_ref[:, cs] = (gelu * v).astype(BF16)
    o_ref[...] = x1_ref[...] + jnp.dot(g_ref[...], wd_ref[...], preferred_element_type=F32)


def _ffn(h2, x1, w_up, conv_w, conv_b, w_down, batch, seq):
    t, d = x1.shape
    tm = ROW_TILE
    n_s = seq // tm
    row = lambda n: pl.BlockSpec((tm, n), lambda b, s: (b * n_s + s, 0))
    full = _resident
    est = (2 * tm * d * (2 + 4 + 4) + 2 * (d * 2 * D_FF + D_FF * d)
           + (tm + 8) * FF_TILE * 4 + tm * D_FF * 2 + 8 * D_FF * 4 + 4 * tm * FF_TILE * 4)
    return pl.pallas_call(
        _ffn_kernel,
        grid=(batch, n_s),
        in_specs=[row(d), row(d), full(w_up), full(conv_w), full(conv_b), full(w_down)],
        out_specs=row(d),
        out_shape=jax.ShapeDtypeStruct((t, d), F32),
        scratch_shapes=[pltpu.VMEM((tm + V7X_SUBLANES, FF_TILE), F32),
                        pltpu.VMEM((D_FF // FF_TILE, V7X_SUBLANES, FF_TILE), F32),
                        pltpu.VMEM((tm, D_FF), BF16)],
        compiler_params=pltpu.CompilerParams(dimension_semantics=("arbitrary",) * 2,
                                             vmem_limit_bytes=_vmem_limit(est)),
        name="ffn",
    )(h2, x1, w_up, conv_w, conv_b, w_down)


def kernel(x, norm1_g, w_in, hg_lb_logits, hg_onorm_g, q_norm_g, k_norm_g, w_a, w_b, w_out,
           norm2_g, w_up, conv_w, conv_b, w_down):
    batch, seq, d = x.shape
    depth = w_in.shape[0]
    n_mix = 4 * HG_WIDTH + 3 * MB_WIDTH
    x2 = x.reshape(batch * seq, d)
    for l in range(depth):
        g1 = norm1_g[l][None, :]
        w_mix = w_in[l][:, :n_mix].astype(BF16)
        w_gate = w_in[l][:, n_mix:].astype(BF16)
        qa, logf, kk, vh, gh, qb, kb, vb = _inproj(
            x2, g1, w_mix, hg_lb_logits, q_norm_g[l], k_norm_g[l], seq, l)
        oa = _hgrn(qa, logf, kk, vh, gh, hg_onorm_g[l], batch, seq)
        ob = _moba(qb, kb, vb, batch, seq)
        x1, h2 = _merge(x2, oa, ob, g1, w_gate, w_a[l].astype(BF16), w_b[l].astype(BF16),
                        w_out[l].astype(BF16), norm2_g[l][None, :])
        x2 = _ffn(h2, x1, w_up[l].astype(BF16), conv_w[l], conv_b[l][None, :],
                  w_down[l].astype(BF16), batch, seq)
    return x2.reshape(batch, seq, d)
```

```python
import functools

import numpy as np
import jax
import jax.numpy as jnp
from jax import lax
from jax.experimental import pallas as pl
from jax.experimental.pallas import tpu as pltpu

F32 = jnp.float32
BF16 = jnp.bfloat16

HG_HEADS = 4
HG_DIM = 128
HG_WIDTH = HG_HEADS * HG_DIM
HG_CHUNK = 64
MB_HEADS = 8
MB_HEAD_DIM = 64
MB_WIDTH = MB_HEADS * MB_HEAD_DIM
MB_BLOCK = 256
MB_TOPK = 3
ROPE_THETA = 10000.0
D_FF = 2816
CONV_WIDTH = 3
NORM_EPS = 1e-6

V7X_LANES = 128
V7X_SUBLANES = 8
V7X_BF16_ROWS = 16
V7X_VMEM_BYTES = 64 * 1024 * 1024

HG_SUB = 16
ROW_TILE = 1024
FF_TILE = 256
NEG_INF = float("-inf")

MB_GROUPS = MB_WIDTH // V7X_LANES
MB_VROWS = MB_HEAD_DIM + V7X_BF16_ROWS

_NT = (((1,), (1,)), ((), ()))
_TN = (((0,), (0,)), ((), ()))


def _vmem_limit(nbytes):
    return int(min(nbytes * 3 // 2 + (4 << 20), V7X_VMEM_BYTES - (4 << 20)))


def _sigmoid(x):
    return 0.5 * jnp.tanh(0.5 * x) + 0.5


def _rms_norm(x, g):
    ms = jnp.mean(x * x, axis=-1, keepdims=True)
    return x * lax.rsqrt(ms + NORM_EPS) * g


def _resident(a):
    return pl.BlockSpec(a.shape, lambda *_: (0,) * a.ndim, pipeline_mode=pl.Buffered(1))


def _store_groups(ref, val):
    for gi in range(ref.shape[0]):
        ref[gi] = val[:, gi * V7X_LANES:(gi + 1) * V7X_LANES].astype(ref.dtype)


def _load_groups(ref):
    return jnp.concatenate([ref[gi] for gi in range(ref.shape[0])], axis=1)


def _inproj_kernel(x_ref, g1_ref, w_ref, lbl_ref, qg_ref, kg_ref, rc_ref, rs_ref, ones_ref,
                   qa_ref, logf_ref, kk_ref, vh_ref, gh_ref, qb_ref, kb_ref, vb_ref, wb_ref, *, layer):
    @pl.when(pl.program_id(0) == 0)
    def _():
        for c0 in range(0, w_ref.shape[1], HG_WIDTH):
            wb_ref[:, c0:c0 + HG_WIDTH] = w_ref[:, c0:c0 + HG_WIDTH].astype(BF16)

    h = _rms_norm(x_ref[...], g1_ref[...]).astype(BF16)

    def proj(c0, width):
        return jnp.dot(h, wb_ref[:, c0:c0 + width], preferred_element_type=F32)

    w = HG_WIDTH
    p = proj(0, w)
    _store_groups(qa_ref, p * _sigmoid(p))
    a = lbl_ref[...]
    e = jnp.exp(a - jnp.max(a, axis=0, keepdims=True))
    lb = jnp.sum(e[0:layer + 1, :], axis=0, keepdims=True) / jnp.sum(e, axis=0, keepdims=True)
    f = lb + (1.0 - lb) * _sigmoid(proj(w, w))
    _store_groups(logf_ref, jnp.log2(f))
    _store_groups(kk_ref, 1.0 - f)
    _store_groups(vh_ref, proj(2 * w, w))
    p = proj(3 * w, w)
    _store_groups(gh_ref, p * _sigmoid(p))

    lane = lax.broadcasted_iota(jnp.int32, (1, V7X_LANES), 1)
    first_half = (lane % MB_HEAD_DIM) < (MB_HEAD_DIM // 2)
    rc = rc_ref[...]
    rs = rs_ref[...]
    ones_bd = ones_ref[...]

    def norm_rope(p, g_ref, out_ref, scale):
        for c in range(MB_GROUPS):
            sl = slice(c * V7X_LANES, (c + 1) * V7X_LANES)
            pc = p[:, sl]
            ss = jnp.dot((pc * pc).astype(BF16), ones_bd, preferred_element_type=F32)
            y = pc * lax.rsqrt(ss * (1.0 / MB_HEAD_DIM) + NORM_EPS) * g_ref[:, sl]
            partner = jnp.where(first_half,
                                pltpu.roll(y, V7X_LANES - MB_HEAD_DIM // 2, 1),
                                pltpu.roll(y, MB_HEAD_DIM // 2, 1))
            out_ref[c] = ((y * rc + partner * rs) * scale).astype(BF16)

    base = 4 * w
    norm_rope(proj(base, MB_WIDTH), qg_ref, qb_ref, float(np.log2(np.e) / np.sqrt(MB_HEAD_DIM)))
    norm_rope(proj(base + MB_WIDTH, MB_WIDTH), kg_ref, kb_ref, 1.0)
    _store_groups(vb_ref, proj(base + 2 * MB_WIDTH, MB_WIDTH))


def _rope_tables(seq):
    half = MB_HEAD_DIM // 2
    inv = 1.0 / (ROPE_THETA ** (jnp.arange(half, dtype=F32) * 2.0 / MB_HEAD_DIM))
    ang = jnp.arange(seq).astype(F32)[:, None] * inv[None, :]
    cos = jnp.cos(ang)
    sin = jnp.sin(ang)
    rc = jnp.concatenate([cos, cos, cos, cos], axis=-1)
    rs = jnp.concatenate([-sin, sin, -sin, sin], axis=-1)
    return rc, rs


def _inproj(x2, g1, w_full, d_in, lb_logits, qg, kg, seq, layer):
    t, d = x2.shape
    tm = ROW_TILE
    rc, rs = _rope_tables(seq)
    blk = np.arange(V7X_LANES) // MB_HEAD_DIM
    ones_bd = jnp.asarray(blk[:, None] == blk[None, :], dtype=BF16)
    qg_t = jnp.tile(qg.astype(F32), MB_HEADS)[None, :]
    kg_t = jnp.tile(kg.astype(F32), MB_HEADS)[None, :]
    n_pos = seq // tm

    full = _resident
    grouped = lambda n, dt: jax.ShapeDtypeStruct((n // V7X_LANES, t, V7X_LANES), dt)
    outs = [grouped(HG_WIDTH, BF16),
            grouped(HG_WIDTH, F32),
            grouped(HG_WIDTH, BF16),
            grouped(HG_WIDTH, BF16),
            grouped(HG_WIDTH, BF16),
            grouped(MB_WIDTH, BF16),
            grouped(MB_WIDTH, BF16),
            grouped(MB_WIDTH, BF16)]
    est = (2 * tm * d * 4 + d * d_in * (4 + 2) + 2 * tm * (d_in * 2 + HG_WIDTH * 4)
           + 4 * tm * V7X_LANES * 4 + 4 * tm * HG_WIDTH * 4)
    w_window = pl.BlockSpec((d, d_in), lambda i: (0, 0), pipeline_mode=pl.Buffered(1))
    return pl.pallas_call(
        functools.partial(_inproj_kernel, layer=layer),
        grid=(t // tm,),
        in_specs=[pl.BlockSpec((tm, d), lambda i: (i, 0)),
                  full(g1), w_window, full(lb_logits), full(qg_t), full(kg_t),
                  pl.BlockSpec((tm, V7X_LANES), lambda i: (i % n_pos, 0)),
                  pl.BlockSpec((tm, V7X_LANES), lambda i: (i % n_pos, 0)),
                  full(ones_bd)],
        out_specs=[pl.BlockSpec((o.shape[0], tm, V7X_LANES), lambda i: (0, i, 0)) for o in outs],
        out_shape=outs,
        scratch_shapes=[pltpu.VMEM((d, d_in), BF16)],
        compiler_params=pltpu.CompilerParams(dimension_semantics=("arbitrary",),
                                             vmem_limit_bytes=_vmem_limit(est)),
        name="inproj",
    )(x2, g1, w_full, lb_logits, qg_t, kg_t, rc, rs, ones_bd)


def _moba_kernel(q_ref, k_ref, v_ref, o_ref, vt_ref, km_ref):
    groups, nb, bk, _ = k_ref.shape
    bq = q_ref.shape[2]
    hd = MB_HEAD_DIM
    heads = V7X_LANES // hd
    n_sel = max(1, min(MB_TOPK, nb - 1))

    ones = jnp.ones((V7X_BF16_ROWS, bk), BF16)
    for gp in range(groups):
        for j in range(nb):
            v_t = v_ref[gp, j].astype(F32).T.astype(BF16)
            for h in range(heads):
                vt_ref[gp, h * MB_VROWS:h * MB_VROWS + hd, j * bk:(j + 1) * bk] = v_t[h * hd:(h + 1) * hd, :]
                vt_ref[gp, h * MB_VROWS + hd:(h + 1) * MB_VROWS, j * bk:(j + 1) * bk] = ones
            km_ref[gp, j:j + 1, :] = jnp.mean(k_ref[gp, j].astype(F32), axis=0, keepdims=True)

    lane = lax.broadcasted_iota(jnp.int32, (1, V7X_LANES), 1)
    causal = (lax.broadcasted_iota(jnp.int32, (bk, bq), 0) <= lax.broadcasted_iota(jnp.int32, (bk, bq), 1))
    blk = lax.broadcasted_iota(jnp.int32, (nb, bq), 0)
    live = {}

    def scores(gp, qi):
        q = q_ref[gp, qi]
        for h in range(heads):
            qh = jnp.where((lane // hd) == h, q, jnp.zeros_like(q))
            bias = None
            if qi > n_sel:
                g = lax.dot_general(km_ref[gp].astype(BF16), qh, _NT, preferred_element_type=F32)
                cnt = jnp.zeros((nb, bq), F32)
                for jp in range(qi):
                    gj = g[jp:jp + 1, :]
                    cnt = cnt + ((gj > g) | ((gj == g) & (jp < blk))).astype(F32)
                bias = jnp.where((cnt < float(n_sel)) & (blk < qi), 0.0, NEG_INF).astype(BF16)
            s_list = []
            for j in range(qi):
                s = lax.dot_general(k_ref[gp, j], qh, _NT, preferred_element_type=F32).astype(BF16)
                s_list.append(s if bias is None else s + bias[j:j + 1, :])
            s_own = lax.dot_general(k_ref[gp, qi], qh, _NT, preferred_element_type=F32).astype(BF16)
            s_list.append(jnp.where(causal, s_own, jnp.full_like(s_own, NEG_INF)))
            live[gp, qi, h] = s_list

    def numerators(gp, qi):
        for h in range(heads):
            s_list = live[gp, qi, h]
            m = jnp.max(functools.reduce(jnp.maximum, s_list), axis=0, keepdims=True)
            live[gp, qi, h] = jnp.concatenate([jnp.exp2(s - m) for s in s_list], axis=0)

    def outputs(gp, qi):
        outs = []
        for h in range(heads):
            acc = jnp.dot(vt_ref[gp, h * MB_VROWS:(h + 1) * MB_VROWS, 0:(qi + 1) * bk], live.pop((gp, qi, h)),
                          preferred_element_type=F32)
            outs.append(acc[0:hd, :] * (1.0 / acc[hd:hd + 1, :]))
        o_ref[gp, qi] = jnp.concatenate(outs, axis=0).T.astype(BF16)

    q_order = list(range(nb - 1, -1, -1))
    for step in range(nb + 2):
        for gp in range(groups):
            if step < nb:
                scores(gp, q_order[step])
            if 0 <= step - 1 < nb:
                numerators(gp, q_order[step - 1])
            if 0 <= step - 2 < nb:
                outputs(gp, q_order[step - 2])


def _moba(qb, kb, vb, batch, seq):
    nb = seq // MB_BLOCK
    r5 = lambda a: a.reshape(MB_GROUPS, batch, nb, MB_BLOCK, V7X_LANES)
    spec = pl.BlockSpec((MB_GROUPS, None, nb, MB_BLOCK, V7X_LANES), lambda b: (0, b, 0, 0, 0))
    est = MB_GROUPS * (2 * 4 * seq * V7X_LANES * 2 + 2 * MB_VROWS * seq * 2 + 4 * seq * MB_BLOCK * 4)
    out = pl.pallas_call(
        _moba_kernel,
        grid=(batch,),
        in_specs=[spec, spec, spec],
        out_specs=spec,
        out_shape=jax.ShapeDtypeStruct((MB_GROUPS, batch, nb, MB_BLOCK, V7X_LANES), BF16),
        scratch_shapes=[pltpu.VMEM((MB_GROUPS, 2 * MB_VROWS, seq), BF16),
                        pltpu.VMEM((MB_GROUPS, nb, V7X_LANES), F32)],
        compiler_params=pltpu.CompilerParams(dimension_semantics=("arbitrary",),
                                             vmem_limit_bytes=_vmem_limit(est)),
        name="moba",
    )(r5(qb), r5(kb), r5(vb))
    return out.reshape(MB_GROUPS, batch * seq, V7X_LANES)


def _hgrn_kernel(qa_ref, logf_ref, kk_ref, v_ref, gate_ref, og_ref, o_ref):
    heads, seq, _ = qa_ref.shape
    ch = HG_CHUNK
    causal = (lax.broadcasted_iota(jnp.int32, (ch, ch), 0) >= lax.broadcasted_iota(jnp.int32, (ch, ch), 1))
    tri = causal.astype(BF16)
    chunks = [slice(g * ch, (g + 1) * ch) for g in range(seq // ch)]
    units = [(hh, cs) for hh in range(heads) for cs in chunks]

    lf = [logf_ref[hh] for hh in range(heads)]
    hi = [x.astype(BF16) for x in lf]
    lo = [(x - h.astype(F32)).astype(BF16) for x, h in zip(lf, hi)]
    qa_h = [qa_ref[hh] for hh in range(heads)]
    kk_h = [kk_ref[hh] for hh in range(heads)]
    v_h = [v_ref[hh] for hh in range(heads)]
    cums = [jnp.dot(tri, hi[hh][cs], preferred_element_type=F32)
            + jnp.dot(tri, lo[hh][cs], preferred_element_type=F32) for hh, cs in units]
    ops = []
    for (hh, cs), cum in zip(units, cums):
        qa, kk = qa_h[hh][cs], kk_h[hh][cs]
        last = cum[ch - 1:ch, :]
        qc = qa * jnp.exp2(cum).astype(BF16)
        kbar = kk * jnp.exp2(last - cum).astype(BF16)
        qts, kts = [], []
        for a in range(ch // HG_SUB):
            lo_r, hi_r = a * HG_SUB, (a + 1) * HG_SUB
            rho = cum[lo_r + HG_SUB // 2 - 1:lo_r + HG_SUB // 2, :]
            qts.append(qa[lo_r:hi_r] * jnp.exp2(cum[lo_r:hi_r] - rho).astype(BF16))
            kt = kk[0:hi_r] * jnp.exp2(rho - cum[0:hi_r]).astype(BF16)
            if hi_r < ch:
                kt = jnp.concatenate([kt, jnp.zeros((ch - hi_r, HG_DIM), BF16)], axis=0)
            kts.append(kt)
        ops.append((qc, kbar, qts, kts, jnp.exp2(last)))
    attns = []
    for qc, kbar, qts, kts, decay in ops:
        a_rows = [lax.dot_general(qt, kt, _NT, preferred_element_type=F32) for qt, kt in zip(qts, kts)]
        attns.append(jnp.where(causal, jnp.concatenate(a_rows, axis=0), 0.0).astype(BF16))
    o_intra = [jnp.dot(attn, v_h[hh][cs], preferred_element_type=F32) for attn, (hh, cs) in zip(attns, units)]
    incs = [lax.dot_general(v_h[hh][cs], op[1], _TN, preferred_element_type=F32) for op, (hh, cs) in zip(ops, units)]
    state = [jnp.zeros((HG_DIM, HG_DIM), F32) for _ in range(heads)]
    for g, cs in enumerate(chunks):
        for hh in range(heads):
            i = hh * len(chunks) + g
            qc, decay = ops[i][0], ops[i][4]
            o = o_intra[i] + lax.dot_general(qc, state[hh].astype(BF16), _NT, preferred_element_type=F32)
            state[hh] = state[hh] * decay + incs[i]
            o_ref[hh, cs, :] = (_rms_norm(o, og_ref[hh]) * gate_ref[hh, cs, :].astype(F32)).astype(BF16)


def _hgrn(qa, logf, kk, vh, gh, og, batch, seq):
    assert seq % HG_CHUNK == 0
    r4 = lambda a: a.reshape(HG_HEADS, batch, seq, HG_DIM)
    spec = pl.BlockSpec((HG_HEADS, None, seq, HG_DIM), lambda b: (0, b, 0, 0))
    og3 = og.reshape(HG_HEADS, 1, HG_DIM)
    est = 2 * HG_HEADS * seq * HG_DIM * (2 * 5 + 4) + 8 * HG_HEADS * seq * HG_DIM * 4
    out = pl.pallas_call(
        _hgrn_kernel,
        grid=(batch,),
        in_specs=[spec, spec, spec, spec, spec, _resident(og3)],
        out_specs=spec,
        out_shape=jax.ShapeDtypeStruct((HG_HEADS, batch, seq, HG_DIM), BF16),
        compiler_params=pltpu.CompilerParams(dimension_semantics=("arbitrary",),
                                             vmem_limit_bytes=_vmem_limit(est)),
        name="hgrn",
    )(r4(qa), r4(logf), r4(kk), r4(vh), r4(gh), og3)
    return out.reshape(HG_HEADS, batch * seq, HG_DIM)


def _merge_kernel(x_ref, oa_ref, ob_ref, g1_ref, wg_ref, wa_ref, wb_ref, wo_ref, g2_ref, x1_ref, h2_ref):
    x = x_ref[...]
    d = x.shape[1]
    ma = jnp.dot(_load_groups(oa_ref), wa_ref[...], preferred_element_type=F32)
    mb = jnp.dot(_load_groups(ob_ref), wb_ref[...], preferred_element_type=F32)
    h1 = _rms_norm(x, g1_ref[...]).astype(BF16)
    ga = jnp.dot(h1, wg_ref[:, 0:d], preferred_element_type=F32)
    gb = jnp.dot(h1, wg_ref[:, d:2 * d], preferred_element_type=F32)
    mix = _sigmoid(ga) * ma + _sigmoid(gb) * mb
    x1 = x + jnp.dot(mix.astype(BF16), wo_ref[...], preferred_element_type=F32)
    x1_ref[...] = x1
    h2_ref[...] = _rms_norm(x1, g2_ref[...]).astype(BF16)


def _merge(x2, oa, ob, g1, w_gate, w_a, w_b, w_out, g2):
    t, d = x2.shape
    tm = ROW_TILE
    row = lambda n: pl.BlockSpec((tm, n), lambda i: (i, 0))
    grouped = lambda a: pl.BlockSpec((a.shape[0], tm, V7X_LANES), lambda i: (0, i, 0))
    full = _resident
    est = (2 * tm * d * (4 + 4 + 2) + 4 * tm * HG_WIDTH * 2
           + 2 * (2 * d * d + 2 * HG_WIDTH * d + d * d) + 4 * tm * d * 4)
    return pl.pallas_call(
        _merge_kernel,
        grid=(t // tm,),
        in_specs=[row(d), grouped(oa), grouped(ob), full(g1), full(w_gate),
                  full(w_a), full(w_b), full(w_out), full(g2)],
        out_specs=[row(d), row(d)],
        out_shape=[jax.ShapeDtypeStruct((t, d), F32), jax.ShapeDtypeStruct((t, d), BF16)],
        compiler_params=pltpu.CompilerParams(dimension_semantics=("arbitrary",),
                                             vmem_limit_bytes=_vmem_limit(est)),
        name="merge",
    )(x2, oa, ob, g1, w_gate, w_a, w_b, w_out, g2)


def _ffn_kernel(h2_ref, x1_ref, wu_ref, cw_ref, cb_ref, wd_ref, o_ref, ubuf_ref, tail_ref, g_ref):
    tm = h2_ref.shape[0]
    pad = V7X_SUBLANES

    @pl.when(pl.program_id(1) == 0)
    def _():
        tail_ref[...] = jnp.zeros_like(tail_ref)

    h2 = h2_ref[...]
    inv_sqrt2 = float(1.0 / np.sqrt(2.0))
    for c in range(D_FF // FF_TILE):
        cs = slice(c * FF_TILE, (c + 1) * FF_TILE)
        u = jnp.dot(h2, wu_ref[:, cs], preferred_element_type=F32)
        v = jnp.dot(h2, wu_ref[:, D_FF + c * FF_TILE:D_FF + (c + 1) * FF_TILE], preferred_element_type=F32)
        ubuf_ref[0:pad, :] = tail_ref[c]
        ubuf_ref[pad:pad + tm, :] = u
        tail_ref[c] = u[tm - pad:tm, :]
        conv = (cb_ref[:, cs]
                + ubuf_ref[pad - 2:pad - 2 + tm, :] * cw_ref[0:1, cs]
                + ubuf_ref[pad - 1:pad - 1 + tm, :] * cw_ref[1:2, cs]
                + u * cw_ref[2:3, cs])
        gelu = 0.5 * conv * (1.0 + lax.erf(conv * inv_sqrt2))
        g_ref[:, cs] = (gelu * v).astype(BF16)
    o_ref[...] = x1_ref[...] + jnp.dot(g_ref[...], wd_ref[...], preferred_element_type=F32)


def _ffn(h2, x1, w_up, conv_w, conv_b, w_down, batch, seq):
    t, d = x1.shape
    tm = ROW_TILE
    n_s = seq // tm
    row = lambda n: pl.BlockSpec((tm, n), lambda b, s: (b * n_s + s, 0))
    full = _resident
    est = (2 * tm * d * (2 + 4 + 4) + 2 * (d * 2 * D_FF + D_FF * d)
           + (tm + 8) * FF_TILE * 4 + tm * D_FF * 2 + 8 * D_FF * 4 + 4 * tm * FF_TILE * 4)
    return pl.pallas_call(
        _ffn_kernel,
        grid=(batch, n_s),
        in_specs=[row(d), row(d), full(w_up), full(conv_w), full(conv_b), full(w_down)],
        out_specs=row(d),
        out_shape=jax.ShapeDtypeStruct((t, d), F32),
        scratch_shapes=[pltpu.VMEM((tm + V7X_SUBLANES, FF_TILE), F32),
                        pltpu.VMEM((D_FF // FF_TILE, V7X_SUBLANES, FF_TILE), F32),
                        pltpu.VMEM((tm, D_FF), BF16)],
        compiler_params=pltpu.CompilerParams(dimension_semantics=("arbitrary",) * 2,
                                             vmem_limit_bytes=_vmem_limit(est)),
        name="ffn",
    )(h2, x1, w_up, conv_w, conv_b, w_down)


def kernel(x, norm1_g, w_in, hg_lb_logits, hg_onorm_g, q_norm_g, k_norm_g, w_a, w_b, w_out,
           norm2_g, w_up, conv_w, conv_b, w_down):
    batch, seq, d = x.shape
    depth = w_in.shape[0]
    n_mix = 4 * HG_WIDTH + 3 * MB_WIDTH
    x2 = x.reshape(batch * seq, d)
    for l in range(depth):
        g1 = norm1_g[l][None, :]
        w_gate = w_in[l][:, n_mix:].astype(BF16)
        qa, logf, kk, vh, gh, qb, kb, vb = _inproj(
            x2, g1, w_in[l], n_mix, hg_lb_logits, q_norm_g[l], k_norm_g[l], seq, l)
        oa = _hgrn(qa, logf, kk, vh, gh, hg_onorm_g[l], batch, seq)
        ob = _moba(qb, kb, vb, batch, seq)
        x1, h2 = _merge(x2, oa, ob, g1, w_gate, w_a[l].astype(BF16), w_b[l].astype(BF16),
                        w_out[l].astype(BF16), norm2_g[l][None, :])
        x2 = _ffn(h2, x1, w_up[l].astype(BF16), conv_w[l], conv_b[l][None, :],
                  w_down[l].astype(BF16), batch, seq)
    return x2.reshape(batch, seq, d)
```
